```python
import math
import jax, jax.numpy as jnp
from jax import lax
import numpy as np

D_MODEL = 1024
BATCH = 8
SEQ = 4096
DEPTH = 2
DEC_BATCH = 128
DEC_SEQ = 4
PAST_LEN = 16384
PAGE_SIZE = 128

HEAD_DIM = 64
N_HEADS_A = 8
N_HEADS_B = 8
N_KV_B = 2
GQA_GROUP = N_HEADS_B // N_KV_B
DILATED_GROUPS = ((128, 1), (512, 4), (2048, 16))
WIN_A = max(w for w, _ in DILATED_GROUPS)
WIN_B = 128
BLOCK = 128
ROT_DIM = HEAD_DIM // 4
ROPE_THETA = 500000.0
EPS = 1e-6
SCALE = HEAD_DIM ** -0.5
W_A = N_HEADS_A * HEAD_DIM
W_B = N_HEADS_B * HEAD_DIM
W_KVB = N_KV_B * HEAD_DIM
MIX_WIDTH = W_A + W_B
IN_COLS = 3 * W_A + W_B + 2 * W_KVB
SPLITS = (W_A, 2 * W_A, 3 * W_A, 3 * W_A + W_B, 3 * W_A + W_B + W_KVB)
D_FF = -(-8 * D_MODEL // (3 * 256)) * 256

kernel_name = "hymba_dilated_swa_sink_decoder_step"


def rms_norm(x, g):
    xf = x.astype(jnp.float32)
    y = xf * lax.rsqrt(jnp.mean(xf * xf, axis=-1, keepdims=True) + EPS)
    return (y * g.astype(jnp.float32)).astype(x.dtype)


def rope_partial(x, pos):
    half = ROT_DIM // 2
    inv = jnp.exp(-math.log(ROPE_THETA) * jnp.arange(0, ROT_DIM, 2, dtype=jnp.float32) / ROT_DIM)
    ang = pos.astype(jnp.float32)[:, None] * inv[None, :]
    cos = jnp.cos(ang)[:, None, :]
    sin = jnp.sin(ang)[:, None, :]
    xr = x[..., :ROT_DIM].astype(jnp.float32)
    x1, x2 = xr[..., :half], xr[..., half:]
    rot = jnp.concatenate([x1 * cos - x2 * sin, x2 * cos + x1 * sin], axis=-1).astype(x.dtype)
    return jnp.concatenate([rot, x[..., ROT_DIM:]], axis=-1)


def mix_inputs(x, g_attn, w_in, qn_a, kn_a, qn_b, kn_b, pos):
    n, l, _ = x.shape
    h = rms_norm(x, g_attn)
    proj = jnp.einsum('nld,de->nle', h, w_in)
    qa, ka, va, qb, kb, vb = jnp.split(proj, SPLITS, axis=-1)
    heads = lambda z, nh: z.reshape(n, l, nh, HEAD_DIM)
    qa = rope_partial(rms_norm(heads(qa, N_HEADS_A), qn_a), pos)
    ka = rope_partial(rms_norm(heads(ka, N_HEADS_A), kn_a), pos)
    qb = rope_partial(rms_norm(heads(qb, N_HEADS_B), qn_b), pos)
    kb = rope_partial(rms_norm(heads(kb, N_KV_B), kn_b), pos)
    return qa, ka, heads(va, N_HEADS_A), qb, kb, heads(vb, N_KV_B)


def softmax_parts(s, sink):
    m = jnp.max(s, axis=-1, keepdims=True)
    if sink is not None:
        m = jnp.maximum(m, sink)
    p = jnp.exp(s - m)
    den = jnp.sum(p, axis=-1, keepdims=True)
    if sink is not None:
        den = den + jnp.exp(sink - m)
    return p, m, den


def banded_attention(q, k, v, max_dist, sink):
    n, l, h, hd = q.shape
    hk = k.shape[2]
    g = h // hk
    lp = -(-l // BLOCK) * BLOCK
    pad = ((0, 0), (0, lp - l), (0, 0), (0, 0))
    q, k, v = (jnp.pad(z, pad) for z in (q, k, v))
    nb = lp // BLOCK
    qb = q.reshape(n, nb, BLOCK, hk, g, hd)

    def with_prev(z):
        zb = z.reshape(n, nb, BLOCK, hk, hd)
        prev = jnp.concatenate([jnp.zeros_like(zb[:, :1]), zb[:, :-1]], axis=1)
        return jnp.concatenate([prev, zb], axis=2)

    k2, v2 = with_prev(k), with_prev(v)
    s = jnp.einsum('nbihgd,nbjhd->nbhgij', qb, k2, preferred_element_type=jnp.float32) * SCALE
    i = jnp.arange(BLOCK)[:, None]
    j = jnp.arange(2 * BLOCK)[None, :]
    dist = i + BLOCK - j
    first = (jnp.arange(nb) == 0)[:, None, None]
    valid = (dist >= 0) & (dist <= max_dist) & ~(first & (j < BLOCK))
    s = jnp.where(valid[None, :, None, None], s, -jnp.inf)
    p, m, den = softmax_parts(s, sink)
    o = jnp.einsum('nbhgij,nbjhd->nbhgid', p, v2.astype(jnp.float32)) / den
    o = o.transpose(0, 1, 4, 2, 3, 5).reshape(n, lp, h, hd)[:, :l]
    lse = (m + jnp.log(den))[..., 0].transpose(0, 1, 4, 2, 3).reshape(n, lp, h)[:, :l]
    return o, lse


def combine_dilations(outs, lses):
    w = jax.nn.softmax(jnp.stack(lses, axis=0), axis=0)
    return jnp.sum(w[..., None] * jnp.stack(outs, axis=0), axis=0)


def dilated_attention_prompt(q, k, v):
    n, s, h, hd = q.shape
    outs, lses = [], []
    for w, d in DILATED_GROUPS:
        lsub = -(-(-(-s // d)) // BLOCK) * BLOCK
        sp = lsub * d

        def split(z):
            z = jnp.pad(z, ((0, 0), (0, sp - s), (0, 0), (0, 0)))
            return z.reshape(n, lsub, d, h, hd).transpose(0, 2, 1, 3, 4).reshape(n * d, lsub, h, hd)

        o, lse = banded_attention(split(q), split(k), split(v), w // d, None)
        outs.append(o.reshape(n, d, lsub, h, hd).transpose(0, 2, 1, 3, 4).reshape(n, sp, h, hd)[:, :s])
        lses.append(lse.reshape(n, d, lsub, h).transpose(0, 2, 1, 3).reshape(n, sp, h)[:, :s])
    return combine_dilations(outs, lses)


def dilated_attention_sample(q, k_all, v_all):
    n, t, h, hd = q.shape
    l = k_all.shape[1] - t
    outs, lses = [], []
    for w, d in DILATED_GROUPS:
        taps = jnp.arange(w // d + 1) * d
        idx = (l + jnp.arange(t))[:, None] - taps[None, :]
        valid = idx >= 0
        idc = jnp.clip(idx, 0)
        kg = jnp.take(k_all, idc, axis=1)
        vg = jnp.take(v_all, idc, axis=1)
        s = jnp.einsum('nthd,ntjhd->nhtj', q, kg, preferred_element_type=jnp.float32) * SCALE
        s = jnp.where(valid[None, None], s, -jnp.inf)
        p, m, den = softmax_parts(s, None)
        o = jnp.einsum('nhtj,ntjhd->nthd', p, vg.astype(jnp.float32)) / den.transpose(0, 2, 1, 3)
        outs.append(o)
        lses.append((m + jnp.log(den))[..., 0].transpose(0, 2, 1))
    return combine_dilations(outs, lses)


def window_attention_sample(q, k_all, v_all, sink):
    n, t, h, hd = q.shape
    hk = k_all.shape[2]
    l = k_all.shape[1] - t
    qg = q.reshape(n, t, hk, h // hk, hd)
    s = jnp.einsum('nthgd,njhd->nhgtj', qg, k_all, preferred_element_type=jnp.float32) * SCALE
    qpos = PAST_LEN + jnp.arange(t)
    kpos = PAST_LEN - l + jnp.arange(l + t)
    dist = qpos[:, None] - kpos[None, :]
    valid = (dist >= 0) & (dist < WIN_B) & (kpos >= 0)[None, :]
    s = jnp.where(valid, s, -jnp.inf)
    p, m, den = softmax_parts(s, sink)
    o = jnp.einsum('nhgtj,njhd->nthgd', p, v_all.astype(jnp.float32)) / den.transpose(0, 3, 1, 2, 4)
    return o.reshape(n, t, h, hd)


def mix_output_and_ffn(x, oa, ob, on_a, on_b, w_out, g_ffn, w_gate_up, w_down):
    n, l, _ = x.shape
    oa = rms_norm(oa.reshape(n, l, W_A), on_a)
    ob = rms_norm(ob.reshape(n, l, W_B), on_b)
    mix = jnp.concatenate([oa, ob], axis=-1).astype(x.dtype)
    x = x + jnp.einsum('nle,ed->nld', mix, w_out)
    h = rms_norm(x, g_ffn)
    gate, up = jnp.split(jnp.einsum('nld,df->nlf', h, w_gate_up), 2, axis=-1)
    return x + jnp.einsum('nlf,fd->nld', jax.nn.silu(gate) * up, w_down)


def setup_inputs(seed: int = 0) -> dict:
    key = jax.random.key(seed)
    ks = jax.random.split(key, 18)
    f32 = jnp.float32
    la = min(WIN_A, PAST_LEN)
    lb = min(WIN_B, PAST_LEN)
    nrm = lambda k, shape, scale=1.0: scale * jax.random.normal(k, shape, f32)
    gain = lambda k, shape: 1.0 + 0.02 * jax.random.normal(k, shape, f32)
    return {
        "x_prompt": nrm(ks[0], (BATCH, SEQ, D_MODEL)),
        "x_sample": nrm(ks[1], (DEC_BATCH, DEC_SEQ, D_MODEL)),
        "cache_a": nrm(ks[2], (DEPTH, DEC_BATCH, la, 2, N_HEADS_A, HEAD_DIM)),
        "cache_b": nrm(ks[3], (DEPTH, DEC_BATCH, lb, 2, N_KV_B, HEAD_DIM)),
        "attn_norm": gain(ks[4], (DEPTH, D_MODEL)),
        "w_in": nrm(ks[5], (DEPTH, D_MODEL, IN_COLS), D_MODEL ** -0.5),
        "q_norm_a": gain(ks[6], (DEPTH, HEAD_DIM)),
        "k_norm_a": gain(ks[7], (DEPTH, HEAD_DIM)),
        "q_norm_b": gain(ks[8], (DEPTH, HEAD_DIM)),
        "k_norm_b": gain(ks[9], (DEPTH, HEAD_DIM)),
        "sinks_b": nrm(ks[10], (DEPTH, N_HEADS_B)),
        "out_norm_a": gain(ks[11], (DEPTH, W_A)),
        "out_norm_b": gain(ks[12], (DEPTH, W_B)),
        "w_out": nrm(ks[13], (DEPTH, MIX_WIDTH, D_MODEL), MIX_WIDTH ** -0.5),
        "ffn_norm": gain(ks[14], (DEPTH, D_MODEL)),
        "w_gate_up": nrm(ks[15], (DEPTH, D_MODEL, 2 * D_FF), D_MODEL ** -0.5),
        "w_down": nrm(ks[16], (DEPTH, D_FF, D_MODEL), D_FF ** -0.5),
    }


def reference(x_prompt, x_sample, cache_a, cache_b, attn_norm, w_in, q_norm_a, k_norm_a, q_norm_b,
              k_norm_b, sinks_b, out_norm_a, out_norm_b, w_out, ffn_norm, w_gate_up, w_down):
    s_len = x_prompt.shape[1]
    t_len = x_sample.shape[1]
    pos_p = jnp.arange(s_len, dtype=jnp.int32)
    pos_s = PAST_LEN + jnp.arange(t_len, dtype=jnp.int32)
    la_p = min(WIN_A, s_len)
    lb_p = min(WIN_B, s_len)
    xp, xs = x_prompt, x_sample
    new_a_p, new_b_p, new_a_s, new_b_s = [], [], [], []
    for layer in range(DEPTH):
        pre = (attn_norm[layer], w_in[layer], q_norm_a[layer], k_norm_a[layer],
               q_norm_b[layer], k_norm_b[layer])
        post = (out_norm_a[layer], out_norm_b[layer], w_out[layer], ffn_norm[layer],
                w_gate_up[layer], w_down[layer])
        sink = sinks_b[layer].astype(jnp.float32).reshape(N_KV_B, GQA_GROUP)[:, :, None, None]

        qa, ka, va, qb, kb, vb = mix_inputs(xp, *pre, pos_p)
        oa = dilated_attention_prompt(qa, ka, va)
        ob, _ = banded_attention(qb, kb, vb, WIN_B - 1, sink)
        new_a_p.append(jnp.stack([ka[:, s_len - la_p:], va[:, s_len - la_p:]], axis=2))
        new_b_p.append(jnp.stack([kb[:, s_len - lb_p:], vb[:, s_len - lb_p:]], axis=2))
        xp = mix_output_and_ffn(xp, oa, ob, *post)

        qa, ka, va, qb, kb, vb = mix_inputs(xs, *pre, pos_s)
        full_a = jnp.concatenate([cache_a[layer], jnp.stack([ka, va], axis=2).astype(cache_a.dtype)], axis=1)
        full_b = jnp.concatenate([cache_b[layer], jnp.stack([kb, vb], axis=2).astype(cache_b.dtype)], axis=1)
        oa = dilated_attention_sample(qa, full_a[:, :, 0], full_a[:, :, 1])
        ob = window_attention_sample(qb, full_b[:, :, 0], full_b[:, :, 1], sink)
        new_a_s.append(full_a[:, t_len:])
        new_b_s.append(full_b[:, t_len:])
        xs = mix_output_and_ffn(xs, oa, ob, *post)
    return (xp, xs, jnp.stack(new_a_p), jnp.stack(new_b_p), jnp.stack(new_a_s), jnp.stack(new_b_s))
```

```python
import functools
import math

import numpy as np
import jax
import jax.numpy as jnp
from jax import lax
from jax.experimental import pallas as pl
from jax.experimental.pallas import tpu as pltpu

D_MODEL = 1024
HEAD_DIM = 64
N_HEADS = 8
N_KV_B = 2
W_MIX = N_HEADS * HEAD_DIM
W_KVB = N_KV_B * HEAD_DIM
D_FF = 2816
DILATIONS = (1, 4, 16)
BAND = 128
WIN_B = 128
ROT_DIM = 16
ROPE_THETA = 500000.0
EPS = 1e-6
SCALE = HEAD_DIM ** -0.5
PAST_LEN = 16384
LANES = 128
COLS = (0, 512, 1024, 1536, 2048, 2176, 2304)
VMEM_LIMIT = 56 * 1024 * 1024
FF_CHUNK = D_FF // 2
PAIRS = W_MIX // LANES
Q_ROWS = 8

_bf16 = jnp.bfloat16
_f32 = jnp.float32


def _params(n_axes):
    return pltpu.CompilerParams(dimension_semantics=("arbitrary",) * n_axes,
                                vmem_limit_bytes=VMEM_LIMIT)


def _const_spec(shape):
    nd = len(shape)
    return pl.BlockSpec(shape, lambda *_: (0,) * nd, pipeline_mode=pl.Buffered(1))


def _lane_lt64(shape):
    return lax.broadcasted_iota(jnp.int32, shape, len(shape) - 1) % LANES < HEAD_DIM


def _rms(x, g):
    return x * lax.rsqrt(jnp.mean(x * x, axis=-1, keepdims=True) + EPS) * g


def _head_norm(y, g2):
    lo = _lane_lt64(y.shape)
    sq = y * y
    s_lo = jnp.sum(jnp.where(lo, sq, 0.0), axis=-1, keepdims=True)
    s_hi = jnp.sum(jnp.where(lo, 0.0, sq), axis=-1, keepdims=True)
    ms = jnp.where(lo, s_lo, s_hi) * (1.0 / HEAD_DIM)
    return y * lax.rsqrt(ms + EPS) * g2


def _rope(y, cos2, sin2):
    lane = lax.broadcasted_iota(jnp.int32, y.shape, 1) % HEAD_DIM
    partner = jnp.where(lane < ROT_DIM // 2, pltpu.roll(y, LANES - ROT_DIM // 2, 1),
                        pltpu.roll(y, ROT_DIM // 2, 1))
    return y * cos2 + partner * sin2


def _rope_tables(pos):
    half = ROT_DIM // 2
    inv = jnp.exp(-math.log(ROPE_THETA) * jnp.arange(0, ROT_DIM, 2, dtype=_f32) / ROT_DIM)
    ang = pos.astype(_f32)[:, None] * inv[None, :]
    cos, sin = jnp.cos(ang), jnp.sin(ang)
    ones = jnp.ones((pos.shape[0], HEAD_DIM - ROT_DIM), _f32)
    c = jnp.concatenate([cos, cos, ones], axis=1)
    s = jnp.concatenate([-sin, sin, 0.0 * ones], axis=1)
    return jnp.concatenate([c, c], axis=1), jnp.concatenate([s, s], axis=1)


def _project(x, g_ref, w_ref, gains, cos2, sin2, scr_ref):
    gqa, gka, gqb, gkb = gains
    h = _rms(x, g_ref[...]).astype(_bf16)

    def seg(i):
        return jnp.dot(h, w_ref[:, COLS[i]:COLS[i + 1]], preferred_element_type=_f32)

    def norm_rope(y, g2, scale):
        outs = []
        for c in range(y.shape[1] // LANES):
            t = _rope(_head_norm(y[:, c * LANES:(c + 1) * LANES], g2), cos2, sin2)
            outs.append(t * scale if scale != 1.0 else t)
        return outs

    for c, t in enumerate(norm_rope(seg(0), gqa[...], SCALE)):
        scr_ref[c] = t
    for c, t in enumerate(norm_rope(seg(1), gka[...], 1.0)):
        scr_ref[PAIRS + c] = t
    va = seg(2)
    for c in range(PAIRS):
        scr_ref[2 * PAIRS + c] = va[:, c * LANES:(c + 1) * LANES]
    qb = jnp.concatenate(norm_rope(seg(3), gqb[...], SCALE), axis=1)
    kb = norm_rope(seg(4), gkb[...], 1.0)[0]
    vb = seg(5)
    return qb, kb, vb


def _dup_heads(y):
    lo = _lane_lt64(y.shape)
    sw = pltpu.roll(y, HEAD_DIM, 1)
    return jnp.concatenate([jnp.where(lo, y, sw), jnp.where(lo, sw, y)], axis=1)


def _proj_prompt_kernel(x_ref, g_ref, w_ref, gqa, gka, gqb, gkb, cos_ref, sin_ref,
                        *rest, tm, n_blocks, ca_blocks):
    (q1, k1, v1, q4, k4, v4, q16, k16, v16, qb_ref, kb_ref, vb_ref, ca_ref, cb_ref, scr_ref) = rest[-15:]
    j = pl.program_id(1)
    qb, kb, vb = _project(x_ref[0], g_ref, w_ref, (gqa, gka, gqb, gkb),
                          cos_ref[...], sin_ref[...], scr_ref)
    qb_ref[0] = qb.astype(_bf16)
    kb_ref[0] = _dup_heads(kb).astype(_bf16)
    vb_ref[0] = _dup_heads(vb).astype(_bf16)
    for d, (qo, ko, vo) in zip(DILATIONS, ((q1, k1, v1), (q4, k4, v4), (q16, k16, v16))):
        for r in range(d):
            rows = pl.ds(r, tm // d, stride=d) if d > 1 else slice(None)
            for i, out in enumerate((qo, ko, vo)):
                for c in range(PAIRS):
                    out[0, r, :, c * LANES:(c + 1) * LANES] = scr_ref[i * PAIRS + c, rows, :].astype(_bf16)

    @pl.when(j >= n_blocks - ca_blocks)
    def _():
        for kv in range(2):
            for c in range(PAIRS):
                ca_ref[0, kv, c * LANES:(c + 1) * LANES, :] = scr_ref[(1 + kv) * PAIRS + c].T

    @pl.when(j == n_blocks - 1)
    def _():
        cb_ref[0, 0] = kb[tm - WIN_B:, :].T
        cb_ref[0, 1] = vb[tm - WIN_B:, :].T


def _proj_prompt(x, g, w, gains, cos2, sin2, cache_shapes, prev, layer, tm=512):
    nb, s, _ = x.shape
    n_blocks = s // tm
    ca_blocks = cache_shapes[0][-1] // tm
    row = lambda n, j: (n, j, 0)
    lay = lambda n, j: (n, 0, j, 0)
    vec = lambda w_: _const_spec((1, w_))
    in_specs = [pl.BlockSpec((1, tm, D_MODEL), row), vec(D_MODEL), _const_spec((D_MODEL, COLS[-1])),
                vec(LANES), vec(LANES), vec(LANES), vec(LANES),
                pl.BlockSpec((tm, LANES), lambda n, j: (j, 0)), pl.BlockSpec((tm, LANES), lambda n, j: (j, 0))]
    args = [x, g, w, *gains, cos2, sin2]
    aliases = {}
    if prev is not None:
        aliases = {len(in_specs): 12, len(in_specs) + 1: 13}
        in_specs += [pl.BlockSpec(memory_space=pl.ANY), pl.BlockSpec(memory_space=pl.ANY)]
        args += list(prev)
    out_shape, out_specs = [], []
    for d in DILATIONS:
        for _ in range(3):
            out_shape.append(jax.ShapeDtypeStruct((nb, d, s // d, W_MIX), _bf16))
            out_specs.append(pl.BlockSpec((1, d, tm // d, W_MIX), lay))
    out_shape += [jax.ShapeDtypeStruct((nb, s, W_MIX), _bf16),
                  jax.ShapeDtypeStruct((nb, s, 2 * W_KVB), _bf16),
                  jax.ShapeDtypeStruct((nb, s, 2 * W_KVB), _bf16),
                  jax.ShapeDtypeStruct(cache_shapes[0], _f32),
                  jax.ShapeDtypeStruct(cache_shapes[1], _f32)]
    first = n_blocks - ca_blocks
    out_specs += [pl.BlockSpec((1, tm, W_MIX), row),
                  pl.BlockSpec((1, tm, 2 * W_KVB), row),
                  pl.BlockSpec((1, tm, 2 * W_KVB), row),
                  pl.BlockSpec((None, 1, 2, W_MIX, tm),
                               lambda n, j: (layer, n, 0, 0, jnp.maximum(j - first, 0))),
                  pl.BlockSpec((None, 1, 2, W_KVB, WIN_B), lambda n, j: (layer, n, 0, 0, 0))]
    return pl.pallas_call(
        functools.partial(_proj_prompt_kernel, tm=tm, n_blocks=n_blocks, ca_blocks=ca_blocks),
        grid=(nb, n_blocks), in_specs=in_specs, out_specs=out_specs, out_shape=out_shape,
        scratch_shapes=[pltpu.VMEM((3 * PAIRS, tm, LANES), _f32)],
        input_output_aliases=aliases,
        compiler_params=_params(2), name=f"proj_prompt_l{layer}",
    )(*args)


def _band_attn_kernel(*refs, d, n_sub, max_dist, swa, with_lse):
    it = iter(refs)
    q_ref, kc_ref, kp_ref, vc_ref, vp_ref = (next(it) for _ in range(5))
    sink_ref = next(it) if swa else None
    o_ref = next(it)
    lse_ref = next(it) if with_lse else None
    kcat, vcat = next(it), next(it)
    c = pl.program_id(1)
    kcat[:, 0:BAND, :] = kp_ref[0]
    kcat[:, BAND:, :] = kc_ref[0]
    vcat[:, 0:BAND, :] = vp_ref[0]
    vcat[:, BAND:, :] = vc_ref[0]

    qi = lax.broadcasted_iota(jnp.int32, (BAND, 2 * BAND), 0)
    kj = lax.broadcasted_iota(jnp.int32, (BAND, 2 * BAND), 1)
    dist = qi + BAND - kj
    band = (dist >= 0) & (dist <= max_dist)
    lo_q = _lane_lt64((BAND, LANES))

    def body(idx, carry):
        r = idx // n_sub
        sb = idx % n_sub
        row0 = pl.multiple_of(sb * BAND, BAND)
        valid = band & (kj >= jnp.where((c == 0) & (sb == 0), BAND, 0))
        q_blk = q_ref[0, r, pl.ds(row0, BAND), :]
        for p in range(N_HEADS // 2):
            kcol = (p // 2) * LANES if swa else p * LANES
            q2 = q_blk[:, p * LANES:(p + 1) * LANES]
            k2 = kcat[r, pl.ds(row0, 2 * BAND), kcol:kcol + LANES]
            v2 = vcat[r, pl.ds(row0, 2 * BAND), kcol:kcol + LANES]
            halves, lses = [], []
            for e in range(2):
                qm = jnp.where(lo_q if e == 0 else jnp.logical_not(lo_q), q2, jnp.zeros_like(q2))
                s = lax.dot_general(qm, k2, (((1,), (1,)), ((), ())), preferred_element_type=_f32)
                s = jnp.where(valid, s, -jnp.inf)
                m = jnp.max(s, axis=-1, keepdims=True)
                if swa:
                    sink = sink_ref[2 * p + e]
                    m = jnp.maximum(m, sink)
                pe = jnp.exp(s - m)
                den = jnp.sum(pe, axis=-1, keepdims=True)
                if swa:
                    den = den + jnp.exp(sink - m)
                acc = jnp.dot(pe.astype(_bf16), v2, preferred_element_type=_f32)
                halves.append(acc * (1.0 / den))
                lses.append(m + jnp.log(den))
            o2 = jnp.where(lo_q, halves[0], halves[1])
            out_rows = (pl.ds(r + d * row0, BAND, stride=d) if d > 1 else pl.ds(row0, BAND))
            o_ref[0, p, out_rows, :] = o2
            if with_lse:
                lse_ref[0, p, out_rows, :] = jnp.where(
                    lo_q, jnp.broadcast_to(lses[0], o2.shape), jnp.broadcast_to(lses[1], o2.shape))
        return carry

    lax.fori_loop(0, d * n_sub, body, 0)


def _band_attn(q, k, v, d, chunk, max_dist, sinks=None, layer=0):
    nb, _, ls, wk = k.shape
    s = ls * d
    rows = chunk // d
    n_sub = rows // BAND
    swa = sinks is not None
    cur = lambda n, c: (n, 0, c, 0)
    prev = lambda n, c: (n, 0, jnp.maximum(c * n_sub - 1, 0), 0)
    in_specs = [pl.BlockSpec((1, d, rows, W_MIX), cur),
                pl.BlockSpec((1, d, rows, wk), cur), pl.BlockSpec((1, d, BAND, wk), prev),
                pl.BlockSpec((1, d, rows, wk), cur), pl.BlockSpec((1, d, BAND, wk), prev)]
    args = [q, k, k, v, v]
    if swa:
        in_specs.append(pl.BlockSpec(memory_space=pltpu.SMEM))
        args.append(sinks)
    nat = pl.BlockSpec((1, PAIRS, chunk, LANES), lambda n, c: (n, 0, c, 0))
    out_shape = [jax.ShapeDtypeStruct((nb, PAIRS, s, LANES), _f32)]
    out_specs = [nat]
    if not swa:
        out_shape.append(jax.ShapeDtypeStruct((nb, PAIRS, s, LANES), _f32))
        out_specs.append(nat)
    return pl.pallas_call(
        functools.partial(_band_attn_kernel, d=d, n_sub=n_sub, max_dist=max_dist, swa=swa,
                          with_lse=not swa),
        grid=(nb, s // chunk), in_specs=in_specs, out_specs=out_specs, out_shape=out_shape,
        scratch_shapes=[pltpu.VMEM((d, rows + BAND, wk), _bf16), pltpu.VMEM((d, rows + BAND, wk), _bf16)],
        compiler_params=_params(2), name=f"attn_d{d}_{'swa' if swa else 'dil'}_l{layer}",
    )(*args)


def _out_ffn_kernel(*refs, n_groups):
    it = iter(refs)
    x_ref = next(it)
    o_refs = [next(it) for _ in range(n_groups)]
    l_refs = [next(it) for _ in range(n_groups)] if n_groups > 1 else []
    ob_ref, ga_ref, gb_ref, wo_ref, gf_ref, wgu_ref, wd_ref, y_ref = (next(it) for _ in range(8))
    x = x_ref[0]
    wide = lambda ref: jnp.concatenate([ref[0, p] for p in range(PAIRS)], axis=1)
    if n_groups > 1:
        lses = [wide(l) for l in l_refs]
        top = functools.reduce(jnp.maximum, lses)
        ws = [jnp.exp(l - top) for l in lses]
        den = functools.reduce(jnp.add, ws)
        oa = functools.reduce(jnp.add, [w * wide(o) for w, o in zip(ws, o_refs)]) / den
    else:
        oa = wide(o_refs[0])
    mix_a = _rms(oa, ga_ref[...]).astype(_bf16)
    mix_b = _rms(wide(ob_ref), gb_ref[...]).astype(_bf16)
    mix = jnp.concatenate([mix_a, mix_b], axis=1)
    x1 = x + jnp.dot(mix, wo_ref[...], preferred_element_type=_f32)
    h = _rms(x1, gf_ref[...]).astype(_bf16)
    y_ref[0] = x1
    for c in range(D_FF // FF_CHUNK):
        lo = c * FF_CHUNK
        gate = jnp.dot(h, wgu_ref[:, lo:lo + FF_CHUNK], preferred_element_type=_f32)
        up = jnp.dot(h, wgu_ref[:, D_FF + lo:D_FF + lo + FF_CHUNK], preferred_element_type=_f32)
        act = (gate * jax.nn.sigmoid(gate) * up).astype(_bf16)
        y_ref[0] += jnp.dot(act, wd_ref[lo:lo + FF_CHUNK, :], preferred_element_type=_f32)


def _out_ffn(x, o_list, lse_list, ob, ga, gb, wo, gf, wgu, wd, tm, name):
    nb, s, _ = x.shape
    row = lambda n, j: (n, j, 0)
    act = lambda w_: pl.BlockSpec((1, tm, w_), row)
    mixer = pl.BlockSpec((1, PAIRS, tm, LANES), lambda n, j: (n, 0, j, 0))
    n_groups = len(o_list)
    in_specs = ([act(D_MODEL)] + [mixer] * (n_groups + len(lse_list) + 1)
                + [_const_spec((1, W_MIX)), _const_spec((1, W_MIX)), _const_spec((D_MODEL, D_MODEL)),
                   _const_spec((1, D_MODEL)), _const_spec((D_MODEL, 2 * D_FF)), _const_spec((D_FF, D_MODEL))])
    return pl.pallas_call(
        functools.partial(_out_ffn_kernel, n_groups=n_groups),
        grid=(nb, s // tm), in_specs=in_specs, out_specs=act(D_MODEL),
        out_shape=jax.ShapeDtypeStruct(x.shape, _f32),
        compiler_params=_params(2), name=name,
    )(x, *o_list, *lse_list, ob, ga, gb, wo, gf, wgu, wd)


def _proj_sample_kernel(x_ref, g_ref, w_ref, gqa, gka, gqb, gkb, cos_ref, sin_ref,
                        qa_ref, qb_ref, kvt_ref, scr_ref):
    qb, kb, vb = _project(x_ref[...], g_ref, w_ref, (gqa, gka, gqb, gkb),
                          cos_ref[...], sin_ref[...], scr_ref)
    qb_ref[...] = qb
    for c in range(PAIRS):
        qa_ref[:, c * LANES:(c + 1) * LANES] = scr_ref[c]
        kvt_ref[c * LANES:(c + 1) * LANES, :] = scr_ref[PAIRS + c].T
        kvt_ref[W_MIX + c * LANES:W_MIX + (c + 1) * LANES, :] = scr_ref[2 * PAIRS + c].T
    kvt_ref[2 * W_MIX:2 * W_MIX + W_KVB, :] = kb.T
    kvt_ref[2 * W_MIX + W_KVB:, :] = vb.T


def _proj_sample(x, g, w, gains, cos2, sin2, layer):
    n_tok = x.shape[0]
    full = lambda shape: pl.BlockSpec(shape, lambda i: (0,) * len(shape))
    in_specs = [full((n_tok, D_MODEL)), full((1, D_MODEL)), full((D_MODEL, COLS[-1]))] + \
               [full((1, LANES))] * 4 + [full((n_tok, LANES))] * 2
    out_shape = [jax.ShapeDtypeStruct((n_tok, W_MIX), _f32), jax.ShapeDtypeStruct((n_tok, W_MIX), _f32),
                 jax.ShapeDtypeStruct((2 * W_MIX + 2 * W_KVB, n_tok), _f32)]
    out_specs = [full((n_tok, W_MIX)), full((n_tok, W_MIX)), full((2 * W_MIX + 2 * W_KVB, n_tok))]
    return pl.pallas_call(
        _proj_sample_kernel, grid=(1,), in_specs=in_specs, out_specs=out_specs, out_shape=out_shape,
        scratch_shapes=[pltpu.VMEM((3 * PAIRS, n_tok, LANES), _f32)],
        compiler_params=_params(1), name=f"proj_sample_l{layer}",
    )(x, g, w, *gains, cos2, sin2)


def _sample_weights(t_len, la, lb):
    def mult(dist):
        m = np.zeros(dist.shape, np.float32)
        for d in DILATIONS:
            m += ((dist >= 0) & (dist % d == 0) & (dist <= BAND * d)).astype(np.float32)
        return m
    t = np.arange(t_len)[:, None]
    w_old = mult(la + t - np.arange(la)[None, :])
    u = np.arange(LANES)[None, :] - (LANES - t_len)
    d_new = np.where(u >= 0, t - u, -1)
    w_new = mult(d_new)
    d_old_b = lb + t - np.arange(lb)[None, :]
    b_old = ((d_old_b >= 0) & (d_old_b < WIN_B)).astype(np.float32)
    b_new = ((d_new >= 0) & (d_new < WIN_B)).astype(np.float32)
    def table(old, new):
        a = np.concatenate([old, new], axis=1)
        return jnp.asarray(np.concatenate([a, np.ones((Q_ROWS - t_len, a.shape[1]), np.float32)]))
    return table(w_old, w_new), table(b_old, b_new)


def _attn_sample_kernel(ca_ref, cb_ref, kvt_ref, qa_ref, qb_ref, wa_ref, wb_ref,
                        sink_ref, *rest, t_len, per_tile):
    na_ref, nb_ref, oa_ref, ob_ref = rest[-4:]
    b = pl.program_id(0)
    la = ca_ref.shape[-1]
    shift = (LANES - t_len) - t_len * (b % per_tile)
    shift = jnp.where(shift < 0, shift + LANES, shift)
    new_t = pltpu.roll(kvt_ref[...], shift, 1)
    tail = lax.broadcasted_iota(jnp.int32, (1, LANES), 1) >= LANES - t_len

    def shifted(old, new_rows):
        rolled = pltpu.roll(old, old.shape[-1] - t_len, 1)
        last = jnp.where(tail, new_rows, rolled[:, -LANES:])
        return rolled, last

    nt = (((1,), (1,)), ((), ()))

    def keys(old_ref, kv, rows, new_base):
        new = new_t[new_base + rows.start:new_base + rows.stop, :]
        return jnp.concatenate([old_ref[kv, rows, :], new], axis=1).astype(_bf16)

    wa = wa_ref[...]
    qa = qa_ref[0]
    lo_a = la - LANES
    for kv in range(2):
        rolled, last = shifted(ca_ref[kv], new_t[kv * W_MIX:(kv + 1) * W_MIX, :])
        na_ref[kv, :, 0:lo_a] = rolled[:, 0:lo_a]
        na_ref[kv, :, lo_a:] = last
    outs = []
    for h in range(N_HEADS):
        hs = slice(h * HEAD_DIM, (h + 1) * HEAD_DIM)
        q = qa[:, hs].astype(_bf16)
        s = jnp.dot(q, keys(ca_ref, 0, hs, 0), preferred_element_type=_f32)
        s = jnp.where(wa > 0, s, -jnp.inf)
        m = jnp.max(s, axis=-1, keepdims=True)
        p = wa * jnp.exp(s - m)
        den = jnp.sum(p, axis=-1, keepdims=True)
        acc = lax.dot_general(p.astype(_bf16), keys(ca_ref, 1, hs, W_MIX), nt, preferred_element_type=_f32)
        outs.append(acc * (1.0 / den))
    oa_ref[0] = jnp.concatenate(outs, axis=1)

    wb = wb_ref[...]
    qb = qb_ref[0]
    base = 2 * W_MIX
    for kv in range(2):
        _, last = shifted(cb_ref[kv], new_t[base + kv * W_KVB:base + (kv + 1) * W_KVB, :])
        nb_ref[kv] = last
    outs = []
    for h in range(N_HEADS):
        g = h // (N_HEADS // N_KV_B)
        gs = slice(g * HEAD_DIM, (g + 1) * HEAD_DIM)
        q = qb[:, h * HEAD_DIM:(h + 1) * HEAD_DIM].astype(_bf16)
        s = jnp.dot(q, keys(cb_ref, 0, gs, base), preferred_element_type=_f32)
        s = jnp.where(wb > 0, s, -jnp.inf)
        sink = sink_ref[h]
        m = jnp.maximum(jnp.max(s, axis=-1, keepdims=True), sink)
        p = jnp.exp(s - m)
        den = jnp.sum(p, axis=-1, keepdims=True) + jnp.exp(sink - m)
        acc = lax.dot_general(p.astype(_bf16), keys(cb_ref, 1, gs, base + W_KVB), nt,
                              preferred_element_type=_f32)
        outs.append(acc * (1.0 / den))
    ob_ref[0] = jnp.concatenate(outs, axis=1)


def _attn_sample(cache_a, cache_b, kvt, qa, qb, weights, sinks, prev, layer, t_len):
    _, nbatch, _, _, la = cache_a.shape
    lb = cache_b.shape[-1]
    per_tile = LANES // t_len
    cur = lambda b: (layer, b, 0, 0, 0)
    tok = lambda b: (b, 0, 0)
    const = lambda shape: pl.BlockSpec(shape, lambda b: (0,) * len(shape))
    in_specs = [pl.BlockSpec((None, None, 2, W_MIX, la), cur),
                pl.BlockSpec((None, None, 2, W_KVB, lb), cur),
                pl.BlockSpec((kvt.shape[0], LANES), lambda b: (0, b // per_tile)),
                pl.BlockSpec((1, Q_ROWS, W_MIX), tok), pl.BlockSpec((1, Q_ROWS, W_MIX), tok),
                const((Q_ROWS, la + LANES)), const((Q_ROWS, lb + LANES)),
                pl.BlockSpec(memory_space=pltpu.SMEM)]
    args = [cache_a, cache_b, kvt, qa, qb, *weights, sinks]
    aliases = {}
    if prev is not None:
        aliases = {len(in_specs): 0, len(in_specs) + 1: 1}
        in_specs += [pl.BlockSpec(memory_space=pl.ANY), pl.BlockSpec(memory_space=pl.ANY)]
        args += list(prev)
    out_shape = [jax.ShapeDtypeStruct(cache_a.shape, _f32), jax.ShapeDtypeStruct(cache_b.shape, _f32),
                 jax.ShapeDtypeStruct((nbatch, Q_ROWS, W_MIX), _f32),
                 jax.ShapeDtypeStruct((nbatch, Q_ROWS, W_MIX), _f32)]
    out_specs = [pl.BlockSpec((None, None, 2, W_MIX, la), cur),
                 pl.BlockSpec((None, None, 2, W_KVB, lb), cur),
                 pl.BlockSpec((1, Q_ROWS, W_MIX), tok), pl.BlockSpec((1, Q_ROWS, W_MIX), tok)]
    return pl.pallas_call(
        functools.partial(_attn_sample_kernel, t_len=t_len, per_tile=per_tile),
        grid=(nbatch,), in_specs=in_specs, out_specs=out_specs, out_shape=out_shape,
        input_output_aliases=aliases,
        compiler_params=_params(1), name=f"attn_sample_l{layer}",
    )(*args)


def _feature_major(c):
    l, b, rows, two, h, hd = c.shape
    return c.transpose(0, 1, 3, 4, 5, 2).reshape(l, b, two, h * hd, rows)


def _row_major(c, h):
    l, b, two, w, rows = c.shape
    return c.reshape(l, b, two, h, w // h, rows).transpose(0, 1, 5, 2, 3, 4)


def kernel(x_prompt, x_sample, cache_a, cache_b, attn_norm, w_in, q_norm_a, k_norm_a, q_norm_b,
           k_norm_b, sinks_b, out_norm_a, out_norm_b, w_out, ffn_norm, w_gate_up, w_down):
    depth = w_in.shape[0]
    nb, s_len, _ = x_prompt.shape
    dec_b, t_len, _ = x_sample.shape
    la_p, lb_p = min(BAND * DILATIONS[-1], s_len), min(WIN_B, s_len)
    assert s_len % (BAND * DILATIONS[-1]) == 0 and la_p == BAND * DILATIONS[-1] and lb_p == WIN_B
    assert LANES % t_len == 0 and (dec_b * t_len) % LANES == 0

    cos_p, sin_p = _rope_tables(jnp.arange(s_len, dtype=jnp.int32))
    pos_s = PAST_LEN + (jnp.arange(dec_b * t_len, dtype=jnp.int32) % t_len)
    cos_s, sin_s = _rope_tables(pos_s)
    ca_fm, cb_fm = _feature_major(cache_a), _feature_major(cache_b)
    la, lb = ca_fm.shape[-1], cb_fm.shape[-1]
    weights = _sample_weights(t_len, la, lb)

    shapes_p = ((depth, nb, 2, W_MIX, la_p), (depth, nb, 2, W_KVB, lb_p))
    prev_p = prev_s = None
    pad_q = lambda q: jnp.pad(q.reshape(dec_b, t_len, W_MIX), ((0, 0), (0, Q_ROWS - t_len), (0, 0)))

    xp = x_prompt
    xs = x_sample.reshape(dec_b * t_len, D_MODEL)
    two = lambda g: jnp.concatenate([g, g])[None, :]
    for layer in range(depth):
        g_attn = attn_norm[layer][None, :]
        w = w_in[layer].astype(_bf16)
        gains = (two(q_norm_a[layer]), two(k_norm_a[layer]), two(q_norm_b[layer]), two(k_norm_b[layer]))
        post = (out_norm_a[layer][None, :], out_norm_b[layer][None, :], w_out[layer].astype(_bf16),
                ffn_norm[layer][None, :], w_gate_up[layer].astype(_bf16), w_down[layer].astype(_bf16))
        sinks = sinks_b[layer]

        (q1, k1, v1, q4, k4, v4, q16, k16, v16, qb, kb, vb, new_a_p, new_b_p) = _proj_prompt(
            xp, g_attn, w, gains, cos_p, sin_p, shapes_p, prev_p, layer)
        prev_p = (new_a_p, new_b_p)
        o_list, lse_list = [], []
        for d, (q, k, v) in zip(DILATIONS, ((q1, k1, v1), (q4, k4, v4), (q16, k16, v16))):
            o, lse = _band_attn(q, k, v, d, max(1024, BAND * d), BAND, layer=layer)
            o_list.append(o)
            lse_list.append(lse)
        (ob,) = _band_attn(qb[:, None], kb[:, None], vb[:, None], 1, 1024, WIN_B - 1, sinks=sinks,
                           layer=layer)
        xp = _out_ffn(xp, o_list, lse_list, ob, *post, tm=512, name=f"out_ffn_prompt_l{layer}")

        qa_s, qb_s, kvt = _proj_sample(xs, g_attn, w, gains, cos_s, sin_s, layer)
        new_a_s, new_b_s, oa_s, ob_s = _attn_sample(
            ca_fm, cb_fm, kvt, pad_q(qa_s), pad_q(qb_s), weights, sinks, prev_s, layer, t_len)
        prev_s = (new_a_s, new_b_s)
        unpad = lambda o: o[:, :t_len].reshape(dec_b * t_len, PAIRS, LANES).transpose(1, 0, 2)[None]
        xs = _out_ffn(xs[None], [unpad(oa_s)], [], unpad(ob_s), *post, tm=dec_b * t_len,
                      name=f"out_ffn_sample_l{layer}")[0]

    return (xp, xs.reshape(dec_b, t_len, D_MODEL), _row_major(new_a_p, N_HEADS), _row_major(new_b_p, N_KV_B),
            _row_major(new_a_s, N_HEADS), _row_major(new_b_s, N_KV_B))
```

```python
import functools
import math

import numpy as np
import jax
import jax.numpy as jnp
from jax import lax
from jax.experimental import pallas as pl
from jax.experimental.pallas import tpu as pltpu

D_MODEL = 1024
HEAD_DIM = 64
N_HEADS = 8
N_KV_B = 2
W_MIX = N_HEADS * HEAD_DIM
W_KVB = N_KV_B * HEAD_DIM
D_FF = 2816
DILATIONS = (1, 4, 16)
BAND = 128
WIN_B = 128
ROT_DIM = 16
ROPE_THETA = 500000.0
EPS = 1e-6
SCALE = HEAD_DIM ** -0.5
PAST_LEN = 16384
LANES = 128
COLS = (0, 512, 1024, 1536, 2048, 2176, 2304)
VMEM_LIMIT = 56 * 1024 * 1024
FF_CHUNK = D_FF // 2
PAIRS = W_MIX // LANES
Q_ROWS = 8

_bf16 = jnp.bfloat16
_f32 = jnp.float32


def _params(n_axes):
    return pltpu.CompilerParams(dimension_semantics=("arbitrary",) * n_axes,
                                vmem_limit_bytes=VMEM_LIMIT)


def _const_spec(shape):
    nd = len(shape)
    return pl.BlockSpec(shape, lambda *_: (0,) * nd, pipeline_mode=pl.Buffered(1))


def _lane_lt64(shape):
    return lax.broadcasted_iota(jnp.int32, shape, len(shape) - 1) % LANES < HEAD_DIM


def _rms(x, g):
    return x * lax.rsqrt(jnp.mean(x * x, axis=-1, keepdims=True) + EPS) * g


def _head_norm(y, g2):
    lo = _lane_lt64(y.shape)
    sq = y * y
    s_lo = jnp.sum(jnp.where(lo, sq, 0.0), axis=-1, keepdims=True)
    s_hi = jnp.sum(jnp.where(lo, 0.0, sq), axis=-1, keepdims=True)
    ms = jnp.where(lo, s_lo, s_hi) * (1.0 / HEAD_DIM)
    return y * lax.rsqrt(ms + EPS) * g2


def _rope(y, cos2, sin2):
    lane = lax.broadcasted_iota(jnp.int32, y.shape, 1)
    swap = jnp.where(lane % HEAD_DIM < ROT_DIM, lane ^ (ROT_DIM // 2), lane)
    partner = jnp.take_along_axis(y, swap, axis=1)
    return y * cos2 + partner * sin2


def _rope_tables(pos):
    half = ROT_DIM // 2
    inv = jnp.exp(-math.log(ROPE_THETA) * jnp.arange(0, ROT_DIM, 2, dtype=_f32) / ROT_DIM)
    ang = pos.astype(_f32)[:, None] * inv[None, :]
    cos, sin = jnp.cos(ang), jnp.sin(ang)
    ones = jnp.ones((pos.shape[0], HEAD_DIM - ROT_DIM), _f32)
    c = jnp.concatenate([cos, cos, ones], axis=1)
    s = jnp.concatenate([-sin, sin, 0.0 * ones], axis=1)
    return jnp.concatenate([c, c], axis=1), jnp.concatenate([s, s], axis=1)


def _project(x, g_ref, w_ref, gains, cos2, sin2, scr_ref):
    gqa, gka, gqb, gkb = gains
    h = _rms(x, g_ref[...]).astype(_bf16)

    def seg(i):
        return jnp.dot(h, w_ref[:, COLS[i]:COLS[i + 1]], preferred_element_type=_f32)

    def norm_rope(y, g2, scale):
        outs = []
        for c in range(y.shape[1] // LANES):
            t = _rope(_head_norm(y[:, c * LANES:(c + 1) * LANES], g2), cos2, sin2)
            outs.append(t * scale if scale != 1.0 else t)
        return outs

    for c, t in enumerate(norm_rope(seg(0), gqa[...], SCALE)):
        scr_ref[c] = t
    for c, t in enumerate(norm_rope(seg(1), gka[...], 1.0)):
        scr_ref[PAIRS + c] = t
    va = seg(2)
    for c in range(PAIRS):
        scr_ref[2 * PAIRS + c] = va[:, c * LANES:(c + 1) * LANES]
    qb = jnp.concatenate(norm_rope(seg(3), gqb[...], SCALE), axis=1)
    kb = norm_rope(seg(4), gkb[...], 1.0)[0]
    vb = seg(5)
    return qb, kb, vb


def _dup_heads(y):
    lo = _lane_lt64(y.shape)
    sw = pltpu.roll(y, HEAD_DIM, 1)
    return jnp.concatenate([jnp.where(lo, y, sw), jnp.where(lo, sw, y)], axis=1)


def _proj_prompt_kernel(x_ref, g_ref, w_ref, gqa, gka, gqb, gkb, cos_ref, sin_ref,
                        *rest, tm, n_blocks, ca_blocks):
    (q1, k1, v1, q4, k4, v4, q16, k16, v16, qb_ref, kb_ref, vb_ref, ca_ref, cb_ref, scr_ref) = rest[-15:]
    j = pl.program_id(1)
    qb, kb, vb = _project(x_ref[0], g_ref, w_ref, (gqa, gka, gqb, gkb),
                          cos_ref[...], sin_ref[...], scr_ref)
    qb_ref[0] = qb.astype(_bf16)
    kb_ref[0] = _dup_heads(kb).astype(_bf16)
    vb_ref[0] = _dup_heads(vb).astype(_bf16)
    for d, (qo, ko, vo) in zip(DILATIONS, ((q1, k1, v1), (q4, k4, v4), (q16, k16, v16))):
        for r in range(d):
            rows = pl.ds(r, tm // d, stride=d) if d > 1 else slice(None)
            for i, out in enumerate((qo, ko, vo)):
                for c in range(PAIRS):
                    out[0, r, :, c * LANES:(c + 1) * LANES] = scr_ref[i * PAIRS + c, rows, :].astype(_bf16)

    @pl.when(j >= n_blocks - ca_blocks)
    def _():
        for kv in range(2):
            for c in range(PAIRS):
                ca_ref[0, kv, c * LANES:(c + 1) * LANES, :] = scr_ref[(1 + kv) * PAIRS + c].T

    @pl.when(j == n_blocks - 1)
    def _():
        cb_ref[0, 0] = kb[tm - WIN_B:, :].T
        cb_ref[0, 1] = vb[tm - WIN_B:, :].T


def _proj_prompt(x, g, w, gains, cos2, sin2, cache_shapes, prev, layer, tm=512):
    nb, s, _ = x.shape
    n_blocks = s // tm
    ca_blocks = cache_shapes[0][-1] // tm
    row = lambda n, j: (n, j, 0)
    lay = lambda n, j: (n, 0, j, 0)
    vec = lambda w_: _const_spec((1, w_))
    in_specs = [pl.BlockSpec((1, tm, D_MODEL), row), vec(D_MODEL), _const_spec((D_MODEL, COLS[-1])),
                vec(LANES), vec(LANES), vec(LANES), vec(LANES),
                pl.BlockSpec((tm, LANES), lambda n, j: (j, 0)), pl.BlockSpec((tm, LANES), lambda n, j: (j, 0))]
    args = [x, g, w, *gains, cos2, sin2]
    aliases = {}
    if prev is not None:
        aliases = {len(in_specs): 12, len(in_specs) + 1: 13}
        in_specs += [pl.BlockSpec(memory_space=pl.ANY), pl.BlockSpec(memory_space=pl.ANY)]
        args += list(prev)
    out_shape, out_specs = [], []
    for d in DILATIONS:
        for _ in range(3):
            out_shape.append(jax.ShapeDtypeStruct((nb, d, s // d, W_MIX), _bf16))
            out_specs.append(pl.BlockSpec((1, d, tm // d, W_MIX), lay))
    out_shape += [jax.ShapeDtypeStruct((nb, s, W_MIX), _bf16),
                  jax.ShapeDtypeStruct((nb, s, 2 * W_KVB), _bf16),
                  jax.ShapeDtypeStruct((nb, s, 2 * W_KVB), _bf16),
                  jax.ShapeDtypeStruct(cache_shapes[0], _f32),
                  jax.ShapeDtypeStruct(cache_shapes[1], _f32)]
    first = n_blocks - ca_blocks
    out_specs += [pl.BlockSpec((1, tm, W_MIX), row),
                  pl.BlockSpec((1, tm, 2 * W_KVB), row),
                  pl.BlockSpec((1, tm, 2 * W_KVB), row),
                  pl.BlockSpec((None, 1, 2, W_MIX, tm),
                               lambda n, j: (layer, n, 0, 0, jnp.maximum(j - first, 0))),
                  pl.BlockSpec((None, 1, 2, W_KVB, WIN_B), lambda n, j: (layer, n, 0, 0, 0))]
    return pl.pallas_call(
        functools.partial(_proj_prompt_kernel, tm=tm, n_blocks=n_blocks, ca_blocks=ca_blocks),
        grid=(nb, n_blocks), in_specs=in_specs, out_specs=out_specs, out_shape=out_shape,
        scratch_shapes=[pltpu.VMEM((3 * PAIRS, tm, LANES), _f32)],
        input_output_aliases=aliases,
        compiler_params=_params(2), name=f"proj_prompt_l{layer}",
    )(*args)


def _band_attn_kernel(*refs, d, n_sub, max_dist, swa, with_lse):
    it = iter(refs)
    q_ref, kc_ref, kp_ref, vc_ref, vp_ref = (next(it) for _ in range(5))
    sink_ref = next(it) if swa else None
    o_ref = next(it)
    lse_ref = next(it) if with_lse else None
    kcat, vcat = next(it), next(it)
    c = pl.program_id(1)
    kcat[:, 0:BAND, :] = kp_ref[0]
    kcat[:, BAND:, :] = kc_ref[0]
    vcat[:, 0:BAND, :] = vp_ref[0]
    vcat[:, BAND:, :] = vc_ref[0]

    qi = lax.broadcasted_iota(jnp.int32, (2 * BAND, BAND), 0) % BAND
    kj = lax.broadcasted_iota(jnp.int32, (2 * BAND, BAND), 1)
    from_prev = kj > qi
    diag = kj == qi
    has_diag = max_dist == BAND
    lo_q = _lane_lt64((BAND, LANES))
    top = lax.broadcasted_iota(jnp.int32, (2 * BAND, 1), 0) < BAND

    def body(idx, carry):
        r = idx // n_sub
        sb = idx % n_sub
        row0 = pl.multiple_of(sb * BAND, BAND)
        prev_bias = jnp.where((c == 0) & (sb == 0), -jnp.inf, 0.0).astype(_f32)
        q_blk = q_ref[0, r, pl.ds(row0, BAND), :]
        kcols = [(p // 2) * LANES if swa else p * LANES for p in range(PAIRS)]
        scores = []
        for p in range(PAIRS):
            q2 = q_blk[:, p * LANES:(p + 1) * LANES]
            zero = jnp.zeros_like(q2)
            qs = jnp.concatenate([jnp.where(lo_q, q2, zero), jnp.where(lo_q, zero, q2)], axis=0)
            k2 = kcat[r, pl.ds(row0, 2 * BAND), kcols[p]:kcols[p] + LANES]
            scores.append(lax.dot_general(qs, k2, (((1,), (1,)), ((), ())), preferred_element_type=_f32))
        soft = []
        for p in range(PAIRS):
            s2 = scores[p]
            s_prev = s2[:, :BAND] + prev_bias
            s = jnp.where(from_prev, s_prev, s2[:, BAND:])
            m = jnp.max(s, axis=-1, keepdims=True)
            if has_diag:
                s_far = jnp.sum(jnp.where(diag, s_prev, 0.0), axis=-1, keepdims=True)
                m = jnp.maximum(m, s_far)
            if swa:
                sink = jnp.where(top, sink_ref[2 * p], sink_ref[2 * p + 1])
                m = jnp.maximum(m, sink)
            pe = jnp.exp(s - m)
            den = jnp.sum(pe, axis=-1, keepdims=True)
            p_prev = jnp.where(from_prev, pe, 0.0)
            if has_diag:
                p_far = jnp.exp(s_far - m)
                den = den + p_far
                p_prev = jnp.where(diag, p_far, p_prev)
            if swa:
                den = den + jnp.exp(sink - m)
            p_cur = jnp.where(from_prev, 0.0, pe)
            soft.append((jnp.concatenate([p_prev, p_cur], axis=1).astype(_bf16), m, den))
        for p in range(PAIRS):
            probs, m, den = soft[p]
            v2 = vcat[r, pl.ds(row0, 2 * BAND), kcols[p]:kcols[p] + LANES]
            acc = jnp.dot(probs, v2, preferred_element_type=_f32) * (1.0 / den)
            out_rows = (pl.ds(r + d * row0, BAND, stride=d) if d > 1 else pl.ds(row0, BAND))
            o_ref[0, p, out_rows, :] = jnp.where(lo_q, acc[:BAND], acc[BAND:])
            if with_lse:
                lse = jnp.broadcast_to(m + jnp.log(den), acc.shape)
                lse_ref[0, p, out_rows, :] = jnp.where(lo_q, lse[:BAND], lse[BAND:])
        return carry

    lax.fori_loop(0, d * n_sub, body, 0)


def _band_attn(q, k, v, d, chunk, max_dist, sinks=None, layer=0):
    assert max_dist in (BAND - 1, BAND)
    nb, _, ls, wk = k.shape
    s = ls * d
    rows = chunk // d
    n_sub = rows // BAND
    swa = sinks is not None
    cur = lambda n, c: (n, 0, c, 0)
    prev = lambda n, c: (n, 0, jnp.maximum(c * n_sub - 1, 0), 0)
    in_specs = [pl.BlockSpec((1, d, rows, W_MIX), cur),
                pl.BlockSpec((1, d, rows, wk), cur), pl.BlockSpec((1, d, BAND, wk), prev),
                pl.BlockSpec((1, d, rows, wk), cur), pl.BlockSpec((1, d, BAND, wk), prev)]
    args = [q, k, k, v, v]
    if swa:
        in_specs.append(pl.BlockSpec(memory_space=pltpu.SMEM))
        args.append(sinks)
    nat = pl.BlockSpec((1, PAIRS, chunk, LANES), lambda n, c: (n, 0, c, 0))
    out_shape = [jax.ShapeDtypeStruct((nb, PAIRS, s, LANES), _f32)]
    out_specs = [nat]
    if not swa:
        out_shape.append(jax.ShapeDtypeStruct((nb, PAIRS, s, LANES), _f32))
        out_specs.append(nat)
    return pl.pallas_call(
        functools.partial(_band_attn_kernel, d=d, n_sub=n_sub, max_dist=max_dist, swa=swa,
                          with_lse=not swa),
        grid=(nb, s // chunk), in_specs=in_specs, out_specs=out_specs, out_shape=out_shape,
        scratch_shapes=[pltpu.VMEM((d, rows + BAND, wk), _bf16), pltpu.VMEM((d, rows + BAND, wk), _bf16)],
        compiler_params=_params(2), name=f"attn_d{d}_{'swa' if swa else 'dil'}_l{layer}",
    )(*args)


def _out_ffn_kernel(*refs, n_groups):
    it = iter(refs)
    x_ref = next(it)
    o_refs = [next(it) for _ in range(n_groups)]
    l_refs = [next(it) for _ in range(n_groups)] if n_groups > 1 else []
    ob_ref, ga_ref, gb_ref, wo_ref, gf_ref, wgu_ref, wd_ref, y_ref = (next(it) for _ in range(8))
    x = x_ref[0]
    wide = lambda ref: jnp.concatenate([ref[0, p] for p in range(PAIRS)], axis=1)
    if n_groups > 1:
        lses = [wide(l) for l in l_refs]
        top = functools.reduce(jnp.maximum, lses)
        ws = [jnp.exp(l - top) for l in lses]
        den = functools.reduce(jnp.add, ws)
        oa = functools.reduce(jnp.add, [w * wide(o) for w, o in zip(ws, o_refs)]) / den
    else:
        oa = wide(o_refs[0])
    mix_a = _rms(oa, ga_ref[...]).astype(_bf16)
    mix_b = _rms(wide(ob_ref), gb_ref[...]).astype(_bf16)
    mix = jnp.concatenate([mix_a, mix_b], axis=1)
    x1 = x + jnp.dot(mix, wo_ref[...], preferred_element_type=_f32)
    h = _rms(x1, gf_ref[...]).astype(_bf16)
    y_ref[0] = x1
    for c in range(D_FF // FF_CHUNK):
        lo = c * FF_CHUNK
        gate = jnp.dot(h, wgu_ref[:, lo:lo + FF_CHUNK], preferred_element_type=_f32)
        up = jnp.dot(h, wgu_ref[:, D_FF + lo:D_FF + lo + FF_CHUNK], preferred_element_type=_f32)
        act = (gate * jax.nn.sigmoid(gate) * up).astype(_bf16)
        y_ref[0] += jnp.dot(act, wd_ref[lo:lo + FF_CHUNK, :], preferred_element_type=_f32)


def _out_ffn(x, o_list, lse_list, ob, ga, gb, wo, gf, wgu, wd, tm, name):
    nb, s, _ = x.shape
    row = lambda n, j: (n, j, 0)
    act = lambda w_: pl.BlockSpec((1, tm, w_), row)
    mixer = pl.BlockSpec((1, PAIRS, tm, LANES), lambda n, j: (n, 0, j, 0))
    n_groups = len(o_list)
    in_specs = ([act(D_MODEL)] + [mixer] * (n_groups + len(lse_list) + 1)
                + [_const_spec((1, W_MIX)), _const_spec((1, W_MIX)), _const_spec((D_MODEL, D_MODEL)),
                   _const_spec((1, D_MODEL)), _const_spec((D_MODEL, 2 * D_FF)), _const_spec((D_FF, D_MODEL))])
    return pl.pallas_call(
        functools.partial(_out_ffn_kernel, n_groups=n_groups),
        grid=(nb, s // tm), in_specs=in_specs, out_specs=act(D_MODEL),
        out_shape=jax.ShapeDtypeStruct(x.shape, _f32),
        compiler_params=_params(2), name=name,
    )(x, *o_list, *lse_list, ob, ga, gb, wo, gf, wgu, wd)


def _proj_sample_kernel(x_ref, g_ref, w_ref, gqa, gka, gqb, gkb, cos_ref, sin_ref,
                        qa_ref, qb_ref, kvt_ref, scr_ref):
    qb, kb, vb = _project(x_ref[...], g_ref, w_ref, (gqa, gka, gqb, gkb),
                          cos_ref[...], sin_ref[...], scr_ref)
    qb_ref[...] = qb
    for c in range(PAIRS):
        qa_ref[:, c * LANES:(c + 1) * LANES] = scr_ref[c]
        kvt_ref[c * LANES:(c + 1) * LANES, :] = scr_ref[PAIRS + c].T
        kvt_ref[W_MIX + c * LANES:W_MIX + (c + 1) * LANES, :] = scr_ref[2 * PAIRS + c].T
    kvt_ref[2 * W_MIX:2 * W_MIX + W_KVB, :] = kb.T
    kvt_ref[2 * W_MIX + W_KVB:, :] = vb.T


def _proj_sample(x, g, w, gains, cos2, sin2, layer):
    n_tok = x.shape[0]
    full = lambda shape: pl.BlockSpec(shape, lambda i: (0,) * len(shape))
    in_specs = [full((n_tok, D_MODEL)), full((1, D_MODEL)), full((D_MODEL, COLS[-1]))] + \
               [full((1, LANES))] * 4 + [full((n_tok, LANES))] * 2
    out_shape = [jax.ShapeDtypeStruct((n_tok, W_MIX), _f32), jax.ShapeDtypeStruct((n_tok, W_MIX), _f32),
                 jax.ShapeDtypeStruct((2 * W_MIX + 2 * W_KVB, n_tok), _f32)]
    out_specs = [full((n_tok, W_MIX)), full((n_tok, W_MIX)), full((2 * W_MIX + 2 * W_KVB, n_tok))]
    return pl.pallas_call(
        _proj_sample_kernel, grid=(1,), in_specs=in_specs, out_specs=out_specs, out_shape=out_shape,
        scratch_shapes=[pltpu.VMEM((3 * PAIRS, n_tok, LANES), _f32)],
        compiler_params=_params(1), name=f"proj_sample_l{layer}",
    )(x, g, w, *gains, cos2, sin2)


def _sample_weights(t_len, la, lb):
    def mult(dist):
        m = np.zeros(dist.shape, np.float32)
        for d in DILATIONS:
            m += ((dist >= 0) & (dist % d == 0) & (dist <= BAND * d)).astype(np.float32)
        return m
    t = np.arange(t_len)[:, None]
    w_old = mult(la + t - np.arange(la)[None, :])
    u = np.arange(LANES)[None, :] - (LANES - t_len)
    d_new = np.where(u >= 0, t - u, -1)
    w_new = mult(d_new)
    d_old_b = lb + t - np.arange(lb)[None, :]
    b_old = ((d_old_b >= 0) & (d_old_b < WIN_B)).astype(np.float32)
    b_new = ((d_new >= 0) & (d_new < WIN_B)).astype(np.float32)
    def table(old, new):
        a = np.concatenate([old, new], axis=1)
        return jnp.asarray(np.concatenate([a, np.ones((Q_ROWS - t_len, a.shape[1]), np.float32)]))
    return table(w_old, w_new), table(b_old, b_new)


def _attn_sample_kernel(ca_ref, cb_ref, kvt_ref, qa_ref, qb_ref, wa_ref, wb_ref,
                        sink_ref, *rest, t_len, per_tile):
    na_ref, nb_ref, oa_ref, ob_ref = rest[-4:]
    b = pl.program_id(0)
    la = ca_ref.shape[-1]
    shift = (LANES - t_len) - t_len * (b % per_tile)
    shift = jnp.where(shift < 0, shift + LANES, shift)
    new_t = pltpu.roll(kvt_ref[...], shift, 1)
    tail = lax.broadcasted_iota(jnp.int32, (1, LANES), 1) >= LANES - t_len

    def shifted(old, new_rows):
        rolled = pltpu.roll(old, old.shape[-1] - t_len, 1)
        last = jnp.where(tail, new_rows, rolled[:, -LANES:])
        return rolled, last

    nt = (((1,), (1,)), ((), ()))

    def keys(old_ref, kv, rows, new_base):
        new = new_t[new_base + rows.start:new_base + rows.stop, :]
        return jnp.concatenate([old_ref[kv, rows, :], new], axis=1).astype(_bf16)

    wa = wa_ref[...]
    qa = qa_ref[0]
    lo_a = la - LANES
    for kv in range(2):
        rolled, last = shifted(ca_ref[kv], new_t[kv * W_MIX:(kv + 1) * W_MIX, :])
        na_ref[kv, :, 0:lo_a] = rolled[:, 0:lo_a]
        na_ref[kv, :, lo_a:] = last
    outs = []
    for h in range(N_HEADS):
        hs = slice(h * HEAD_DIM, (h + 1) * HEAD_DIM)
        q = qa[:, hs].astype(_bf16)
        s = jnp.dot(q, keys(ca_ref, 0, hs, 0), preferred_element_type=_f32)
        s = jnp.where(wa > 0, s, -jnp.inf)
        m = jnp.max(s, axis=-1, keepdims=True)
        p = wa * jnp.exp(s - m)
        den = jnp.sum(p, axis=-1, keepdims=True)
        acc = lax.dot_general(p.astype(_bf16), keys(ca_ref, 1, hs, W_MIX), nt, preferred_element_type=_f32)
        outs.append(acc * (1.0 / den))
    oa_ref[0] = jnp.concatenate(outs, axis=1)

    wb = wb_ref[...]
    qb = qb_ref[0]
    base = 2 * W_MIX
    for kv in range(2):
        _, last = shifted(cb_ref[kv], new_t[base + kv * W_KVB:base + (kv + 1) * W_KVB, :])
        nb_ref[kv] = last
    outs = []
    for h in range(N_HEADS):
        g = h // (N_HEADS // N_KV_B)
        gs = slice(g * HEAD_DIM, (g + 1) * HEAD_DIM)
        q = qb[:, h * HEAD_DIM:(h + 1) * HEAD_DIM].astype(_bf16)
        s = jnp.dot(q, keys(cb_ref, 0, gs, base), preferred_element_type=_f32)
        s = jnp.where(wb > 0, s, -jnp.inf)
        sink = sink_ref[h]
        m = jnp.maximum(jnp.max(s, axis=-1, keepdims=True), sink)
        p = jnp.exp(s - m)
        den = jnp.sum(p, axis=-1, keepdims=True) + jnp.exp(sink - m)
        acc = lax.dot_general(p.astype(_bf16), keys(cb_ref, 1, gs, base + W_KVB), nt,
                              preferred_element_type=_f32)
        outs.append(acc * (1.0 / den))
    ob_ref[0] = jnp.concatenate(outs, axis=1)


def _attn_sample(cache_a, cache_b, kvt, qa, qb, weights, sinks, prev, layer, t_len):
    _, nbatch, _, _, la = cache_a.shape
    lb = cache_b.shape[-1]
    per_tile = LANES // t_len
    cur = lambda b: (layer, b, 0, 0, 0)
    tok = lambda b: (b, 0, 0)
    const = lambda shape: pl.BlockSpec(shape, lambda b: (0,) * len(shape))
    in_specs = [pl.BlockSpec((None, None, 2, W_MIX, la), cur),
                pl.BlockSpec((None, None, 2, W_KVB, lb), cur),
                pl.BlockSpec((kvt.shape[0], LANES), lambda b: (0, b // per_tile)),
                pl.BlockSpec((1, Q_ROWS, W_MIX), tok), pl.BlockSpec((1, Q_ROWS, W_MIX), tok),
                const((Q_ROWS, la + LANES)), const((Q_ROWS, lb + LANES)),
                pl.BlockSpec(memory_space=pltpu.SMEM)]
    args = [cache_a, cache_b, kvt, qa, qb, *weights, sinks]
    aliases = {}
    if prev is not None:
        aliases = {len(in_specs): 0, len(in_specs) + 1: 1}
        in_specs += [pl.BlockSpec(memory_space=pl.ANY), pl.BlockSpec(memory_space=pl.ANY)]
        args += list(prev)
    out_shape = [jax.ShapeDtypeStruct(cache_a.shape, _f32), jax.ShapeDtypeStruct(cache_b.shape, _f32),
                 jax.ShapeDtypeStruct((nbatch, Q_ROWS, W_MIX), _f32),
                 jax.ShapeDtypeStruct((nbatch, Q_ROWS, W_MIX), _f32)]
    out_specs = [pl.BlockSpec((None, None, 2, W_MIX, la), cur),
                 pl.BlockSpec((None, None, 2, W_KVB, lb), cur),
                 pl.BlockSpec((1, Q_ROWS, W_MIX), tok), pl.BlockSpec((1, Q_ROWS, W_MIX), tok)]
    return pl.pallas_call(
        functools.partial(_attn_sample_kernel, t_len=t_len, per_tile=per_tile),
        grid=(nbatch,), in_specs=in_specs, out_specs=out_specs, out_shape=out_shape,
        input_output_aliases=aliases,
        compiler_params=_params(1), name=f"attn_sample_l{layer}",
    )(*args)


def _feature_major(c):
    l, b, rows, two, h, hd = c.shape
    return c.transpose(0, 1, 3, 4, 5, 2).reshape(l, b, two, h * hd, rows)


def _row_major(c, h):
    l, b, two, w, rows = c.shape
    return c.reshape(l, b, two, h, w // h, rows).transpose(0, 1, 5, 2, 3, 4)


def kernel(x_prompt, x_sample, cache_a, cache_b, attn_norm, w_in, q_norm_a, k_norm_a, q_norm_b,
           k_norm_b, sinks_b, out_norm_a, out_norm_b, w_out, ffn_norm, w_gate_up, w_down):
    depth = w_in.shape[0]
    nb, s_len, _ = x_prompt.shape
    dec_b, t_len, _ = x_sample.shape
    la_p, lb_p = min(BAND * DILATIONS[-1], s_len), min(WIN_B, s_len)
    assert s_len % (BAND * DILATIONS[-1]) == 0 and la_p == BAND * DILATIONS[-1] and lb_p == WIN_B
    assert LANES % t_len == 0 and (dec_b * t_len) % LANES == 0

    cos_p, sin_p = _rope_tables(jnp.arange(s_len, dtype=jnp.int32))
    pos_s = PAST_LEN + (jnp.arange(dec_b * t_len, dtype=jnp.int32) % t_len)
    cos_s, sin_s = _rope_tables(pos_s)
    ca_fm, cb_fm = _feature_major(cache_a), _feature_major(cache_b)
    la, lb = ca_fm.shape[-1], cb_fm.shape[-1]
    weights = _sample_weights(t_len, la, lb)

    shapes_p = ((depth, nb, 2, W_MIX, la_p), (depth, nb, 2, W_KVB, lb_p))
    prev_p = prev_s = None
    pad_q = lambda q: jnp.pad(q.reshape(dec_b, t_len, W_MIX), ((0, 0), (0, Q_ROWS - t_len), (0, 0)))

    xp = x_prompt
    xs = x_sample.reshape(dec_b * t_len, D_MODEL)
    two = lambda g: jnp.concatenate([g, g])[None, :]
    for layer in range(depth):
        g_attn = attn_norm[layer][None, :]
        w = w_in[layer].astype(_bf16)
        gains = (two(q_norm_a[layer]), two(k_norm_a[layer]), two(q_norm_b[layer]), two(k_norm_b[layer]))
        post = (out_norm_a[layer][None, :], out_norm_b[layer][None, :], w_out[layer].astype(_bf16),
                ffn_norm[layer][None, :], w_gate_up[layer].astype(_bf16), w_down[layer].astype(_bf16))
        sinks = sinks_b[layer]

        (q1, k1, v1, q4, k4, v4, q16, k16, v16, qb, kb, vb, new_a_p, new_b_p) = _proj_prompt(
            xp, g_attn, w, gains, cos_p, sin_p, shapes_p, prev_p, layer)
        prev_p = (new_a_p, new_b_p)
        o_list, lse_list = [], []
        for d, (q, k, v) in zip(DILATIONS, ((q1, k1, v1), (q4, k4, v4), (q16, k16, v16))):
            o, lse = _band_attn(q, k, v, d, max(1024, BAND * d), BAND, layer=layer)
            o_list.append(o)
            lse_list.append(lse)
        (ob,) = _band_attn(qb[:, None], kb[:, None], vb[:, None], 1, 1024, WIN_B - 1, sinks=sinks,
                           layer=layer)
        xp = _out_ffn(xp, o_list, lse_list, ob, *post, tm=512, name=f"out_ffn_prompt_l{layer}")

        qa_s, qb_s, kvt = _proj_sample(xs, g_attn, w, gains, cos_s, sin_s, layer)
        new_a_s, new_b_s, oa_s, ob_s = _attn_sample(
            ca_fm, cb_fm, kvt, pad_q(qa_s), pad_q(qb_s), weights, sinks, prev_s, layer, t_len)
        prev_s = (new_a_s, new_b_s)
        unpad = lambda o: o[:, :t_len].reshape(dec_b * t_len, PAIRS, LANES).transpose(1, 0, 2)[None]
        xs = _out_ffn(xs[None], [unpad(oa_s)], [], unpad(ob_s), *post, tm=dec_b * t_len,
                      name=f"out_ffn_sample_l{layer}")[0]

    return (xp, xs.reshape(dec_b, t_len, D_MODEL), _row_major(new_a_p, N_HEADS), _row_major(new_b_p, N_KV_B),
            _row_major(new_a_s, N_HEADS), _row_major(new_b_s, N_KV_B))
```

```python
import functools
import math

import numpy as np
import jax
import jax.numpy as jnp
from jax import lax
from jax.experimental import pallas as pl
from jax.experimental.pallas import tpu as pltpu

D_MODEL = 1024
HEAD_DIM = 64
N_HEADS = 8
N_KV_B = 2
W_MIX = N_HEADS * HEAD_DIM
W_KVB = N_KV_B * HEAD_DIM
D_FF = 2816
DILATIONS = (1, 4, 16)
BAND = 128
WIN_B = 128
ROT_DIM = 16
ROPE_THETA = 500000.0
EPS = 1e-6
SCALE = HEAD_DIM ** -0.5
PAST_LEN = 16384
LANES = 128
COLS = (0, 512, 1024, 1536, 2048, 2176, 2304)
VMEM_LIMIT = 60 * 1024 * 1024
FF_CHUNK = D_FF // 2
FF_SPLIT = 768
PAIRS = W_MIX // LANES
Q_ROWS = 8

_bf16 = jnp.bfloat16
_f32 = jnp.float32


def _params(n_axes):
    return pltpu.CompilerParams(dimension_semantics=("arbitrary",) * n_axes,
                                vmem_limit_bytes=VMEM_LIMIT)


def _const_spec(shape):
    nd = len(shape)
    return pl.BlockSpec(shape, lambda *_: (0,) * nd, pipeline_mode=pl.Buffered(1))


def _lane_lt64(shape):
    return lax.broadcasted_iota(jnp.int32, shape, len(shape) - 1) % LANES < HEAD_DIM


def _rms(x, g):
    return x * lax.rsqrt(jnp.mean(x * x, axis=-1, keepdims=True) + EPS) * g


def _head_norm(y, g2):
    lo = _lane_lt64(y.shape)
    sq = y * y
    s_lo = jnp.sum(jnp.where(lo, sq, 0.0), axis=-1, keepdims=True)
    s_hi = jnp.sum(jnp.where(lo, 0.0, sq), axis=-1, keepdims=True)
    ms = jnp.where(lo, s_lo, s_hi) * (1.0 / HEAD_DIM)
    return y * lax.rsqrt(ms + EPS) * g2


def _rope(y, cos2, sin2):
    lane = lax.broadcasted_iota(jnp.int32, y.shape, 1)
    swap = jnp.where(lane % HEAD_DIM < ROT_DIM, lane ^ (ROT_DIM // 2), lane)
    partner = jnp.take_along_axis(y, swap, axis=1)
    return y * cos2 + partner * sin2


def _rope_tables(pos):
    half = ROT_DIM // 2
    inv = jnp.exp(-math.log(ROPE_THETA) * jnp.arange(0, ROT_DIM, 2, dtype=_f32) / ROT_DIM)
    ang = pos.astype(_f32)[:, None] * inv[None, :]
    cos, sin = jnp.cos(ang), jnp.sin(ang)
    ones = jnp.ones((pos.shape[0], HEAD_DIM - ROT_DIM), _f32)
    c = jnp.concatenate([cos, cos, ones], axis=1)
    s = jnp.concatenate([-sin, sin, 0.0 * ones], axis=1)
    return jnp.concatenate([c, c], axis=1), jnp.concatenate([s, s], axis=1)


def _project(x, g_ref, w_ref, gains, cos2, sin2, scr_ref):
    gqa, gka, gqb, gkb = gains
    h = _rms(x, g_ref[...]).astype(_bf16)

    def seg(i):
        return jnp.dot(h, w_ref[:, COLS[i]:COLS[i + 1]], preferred_element_type=_f32)

    def norm_rope(y, g2, scale):
        outs = []
        for c in range(y.shape[1] // LANES):
            t = _rope(_head_norm(y[:, c * LANES:(c + 1) * LANES], g2), cos2, sin2)
            outs.append(t * scale if scale != 1.0 else t)
        return outs

    segs = [seg(i) for i in range(len(COLS) - 1)]
    for c, t in enumerate(norm_rope(segs[0], gqa[...], SCALE)):
        scr_ref[c] = t
    for c, t in enumerate(norm_rope(segs[1], gka[...], 1.0)):
        scr_ref[PAIRS + c] = t
    for c in range(PAIRS):
        scr_ref[2 * PAIRS + c] = segs[2][:, c * LANES:(c + 1) * LANES]
    qb = jnp.concatenate(norm_rope(segs[3], gqb[...], SCALE), axis=1)
    kb = norm_rope(segs[4], gkb[...], 1.0)[0]
    return qb, kb, segs[5]


def _dup_heads(y):
    lo = _lane_lt64(y.shape)
    sw = pltpu.roll(y, HEAD_DIM, 1)
    return jnp.concatenate([jnp.where(lo, y, sw), jnp.where(lo, sw, y)], axis=1)


def _proj_prompt_kernel(x_ref, g_ref, w_ref, gqa, gka, gqb, gkb, cos_ref, sin_ref,
                        *rest, tm, n_blocks, ca_blocks):
    (q1, k1, v1, q4, k4, v4, q16, k16, v16, qb_ref, kb_ref, vb_ref, ca_ref, cb_ref, scr_ref) = rest[-15:]
    j = pl.program_id(1)
    qb, kb, vb = _project(x_ref[0], g_ref, w_ref, (gqa, gka, gqb, gkb),
                          cos_ref[...], sin_ref[...], scr_ref)
    qb_ref[0] = qb.astype(_bf16)
    kb_ref[0] = _dup_heads(kb).astype(_bf16)
    vb_ref[0] = _dup_heads(vb).astype(_bf16)
    for d, (qo, ko, vo) in zip(DILATIONS, ((q1, k1, v1), (q4, k4, v4), (q16, k16, v16))):
        for r in range(d):
            rows = pl.ds(r, tm // d, stride=d) if d > 1 else slice(None)
            for i, out in enumerate((qo, ko, vo)):
                for c in range(PAIRS):
                    out[0, r, :, c * LANES:(c + 1) * LANES] = scr_ref[i * PAIRS + c, rows, :].astype(_bf16)

    @pl.when(j >= n_blocks - ca_blocks)
    def _():
        for kv in range(2):
            for c in range(PAIRS):
                ca_ref[0, kv, c * LANES:(c + 1) * LANES, :] = scr_ref[(1 + kv) * PAIRS + c].T

    @pl.when(j == n_blocks - 1)
    def _():
        cb_ref[0, 0] = kb[tm - WIN_B:, :].T
        cb_ref[0, 1] = vb[tm - WIN_B:, :].T


def _proj_prompt(x, g, w, gains, cos2, sin2, cache_shapes, prev, layer, tm=512):
    nb, s, _ = x.shape
    n_blocks = s // tm
    ca_blocks = cache_shapes[0][-1] // tm
    row = lambda n, j: (n, j, 0)
    lay = lambda n, j: (n, 0, j, 0)
    vec = lambda w_: _const_spec((1, w_))
    in_specs = [pl.BlockSpec((1, tm, D_MODEL), row), vec(D_MODEL), _const_spec((D_MODEL, COLS[-1])),
                vec(LANES), vec(LANES), vec(LANES), vec(LANES),
                pl.BlockSpec((tm, LANES), lambda n, j: (j, 0)), pl.BlockSpec((tm, LANES), lambda n, j: (j, 0))]
    args = [x, g, w, *gains, cos2, sin2]
    aliases = {}
    if prev is not None:
        aliases = {len(in_specs): 12, len(in_specs) + 1: 13}
        in_specs += [pl.BlockSpec(memory_space=pl.ANY), pl.BlockSpec(memory_space=pl.ANY)]
        args += list(prev)
    out_shape, out_specs = [], []
    for d in DILATIONS:
        for _ in range(3):
            out_shape.append(jax.ShapeDtypeStruct((nb, d, s // d, W_MIX), _bf16))
            out_specs.append(pl.BlockSpec((1, d, tm // d, W_MIX), lay))
    out_shape += [jax.ShapeDtypeStruct((nb, s, W_MIX), _bf16),
                  jax.ShapeDtypeStruct((nb, s, 2 * W_KVB), _bf16),
                  jax.ShapeDtypeStruct((nb, s, 2 * W_KVB), _bf16),
                  jax.ShapeDtypeStruct(cache_shapes[0], _f32),
                  jax.ShapeDtypeStruct(cache_shapes[1], _f32)]
    first = n_blocks - ca_blocks
    out_specs += [pl.BlockSpec((1, tm, W_MIX), row),
                  pl.BlockSpec((1, tm, 2 * W_KVB), row),
                  pl.BlockSpec((1, tm, 2 * W_KVB), row),
                  pl.BlockSpec((None, 1, 2, W_MIX, tm),
                               lambda n, j: (layer, n, 0, 0, jnp.maximum(j - first, 0))),
                  pl.BlockSpec((None, 1, 2, W_KVB, WIN_B), lambda n, j: (layer, n, 0, 0, 0))]
    return pl.pallas_call(
        functools.partial(_proj_prompt_kernel, tm=tm, n_blocks=n_blocks, ca_blocks=ca_blocks),
        grid=(nb, n_blocks), in_specs=in_specs, out_specs=out_specs, out_shape=out_shape,
        scratch_shapes=[pltpu.VMEM((3 * PAIRS, tm, LANES), _f32)],
        input_output_aliases=aliases,
        compiler_params=_params(2), name=f"proj_prompt_l{layer}",
    )(*args)


def _band_attn_kernel(*refs, d, n_sub, max_dist, swa, with_lse):
    it = iter(refs)
    q_ref, kc_ref, kp_ref, vc_ref, vp_ref = (next(it) for _ in range(5))
    sink_ref = next(it) if swa else None
    o_ref = next(it)
    lse_ref = next(it) if with_lse else None
    kcat, vcat = next(it), next(it)
    c = pl.program_id(1)
    kcat[:, 0:BAND, :] = kp_ref[0]
    kcat[:, BAND:, :] = kc_ref[0]
    vcat[:, 0:BAND, :] = vp_ref[0]
    vcat[:, BAND:, :] = vc_ref[0]

    qi = lax.broadcasted_iota(jnp.int32, (2 * BAND, BAND), 0) % BAND
    kj = lax.broadcasted_iota(jnp.int32, (2 * BAND, BAND), 1)
    from_prev = kj > qi
    diag = kj == qi
    has_diag = max_dist == BAND
    lo_q = _lane_lt64((BAND, LANES))
    top = lax.broadcasted_iota(jnp.int32, (2 * BAND, 1), 0) < BAND

    def body(idx, carry):
        r = idx // n_sub
        sb = idx % n_sub
        row0 = pl.multiple_of(sb * BAND, BAND)
        prev_bias = jnp.where((c == 0) & (sb == 0), -jnp.inf, 0.0).astype(_f32)
        q_blk = q_ref[0, r, pl.ds(row0, BAND), :]
        kcols = [(p // 2) * LANES if swa else p * LANES for p in range(PAIRS)]
        scores = []
        for p in range(PAIRS):
            q2 = q_blk[:, p * LANES:(p + 1) * LANES]
            zero = jnp.zeros_like(q2)
            qs = jnp.concatenate([jnp.where(lo_q, q2, zero), jnp.where(lo_q, zero, q2)], axis=0)
            k2 = kcat[r, pl.ds(row0, 2 * BAND), kcols[p]:kcols[p] + LANES]
            scores.append(lax.dot_general(qs, k2, (((1,), (1,)), ((), ())), preferred_element_type=_f32))
        soft = []
        for p in range(PAIRS):
            s2 = scores[p]
            s_prev = s2[:, :BAND] + prev_bias
            s = jnp.where(from_prev, s_prev, s2[:, BAND:])
            m = jnp.max(s, axis=-1, keepdims=True)
            if has_diag:
                s_far = jnp.sum(jnp.where(diag, s_prev, 0.0), axis=-1, keepdims=True)
                m = jnp.maximum(m, s_far)
            if swa:
                sink = jnp.where(top, sink_ref[2 * p], sink_ref[2 * p + 1])
                m = jnp.maximum(m, sink)
            pe = jnp.exp(s - m)
            den = jnp.sum(pe, axis=-1, keepdims=True)
            p_prev = jnp.where(from_prev, pe, 0.0)
            if has_diag:
                p_far = jnp.exp(s_far - m)
                den = den + p_far
                p_prev = jnp.where(diag, p_far, p_prev)
            if swa:
                den = den + jnp.exp(sink - m)
            p_cur = jnp.where(from_prev, 0.0, pe)
            soft.append((jnp.concatenate([p_prev, p_cur], axis=1).astype(_bf16), m, den))
        for p in range(PAIRS):
            probs, m, den = soft[p]
            v2 = vcat[r, pl.ds(row0, 2 * BAND), kcols[p]:kcols[p] + LANES]
            acc = jnp.dot(probs, v2, preferred_element_type=_f32) * (1.0 / den)
            out_rows = (pl.ds(r + d * row0, BAND, stride=d) if d > 1 else pl.ds(row0, BAND))
            o_ref[0, p, out_rows, :] = jnp.where(lo_q, acc[:BAND], acc[BAND:])
            if with_lse:
                lse = jnp.broadcast_to(m + jnp.log(den), acc.shape)
                lse_ref[0, p, out_rows, :] = jnp.where(lo_q, lse[:BAND], lse[BAND:])
        return carry

    lax.fori_loop(0, d * n_sub, body, 0)


def _band_attn(q, k, v, d, chunk, max_dist, sinks=None, layer=0):
    assert max_dist in (BAND - 1, BAND)
    nb, _, ls, wk = k.shape
    s = ls * d
    rows = chunk // d
    n_sub = rows // BAND
    swa = sinks is not None
    cur = lambda n, c: (n, 0, c, 0)
    prev = lambda n, c: (n, 0, jnp.maximum(c * n_sub - 1, 0), 0)
    in_specs = [pl.BlockSpec((1, d, rows, W_MIX), cur),
                pl.BlockSpec((1, d, rows, wk), cur), pl.BlockSpec((1, d, BAND, wk), prev),
                pl.BlockSpec((1, d, rows, wk), cur), pl.BlockSpec((1, d, BAND, wk), prev)]
    args = [q, k, k, v, v]
    if swa:
        in_specs.append(pl.BlockSpec(memory_space=pltpu.SMEM))
        args.append(sinks)
    nat = pl.BlockSpec((1, PAIRS, chunk, LANES), lambda n, c: (n, 0, c, 0))
    out_shape = [jax.ShapeDtypeStruct((nb, PAIRS, s, LANES), _f32)]
    out_specs = [nat]
    if not swa:
        out_shape.append(jax.ShapeDtypeStruct((nb, PAIRS, s, LANES), _f32))
        out_specs.append(nat)
    return pl.pallas_call(
        functools.partial(_band_attn_kernel, d=d, n_sub=n_sub, max_dist=max_dist, swa=swa,
                          with_lse=not swa),
        grid=(nb, s // chunk), in_specs=in_specs, out_specs=out_specs, out_shape=out_shape,
        scratch_shapes=[pltpu.VMEM((d, rows + BAND, wk), _bf16), pltpu.VMEM((d, rows + BAND, wk), _bf16)],
        compiler_params=_params(2), name=f"attn_d{d}_{'swa' if swa else 'dil'}_l{layer}",
    )(*args)


def _mix_residual(x, o_refs, l_refs, ob_ref, ga_ref, gb_ref, wo_ref):
    wide = lambda ref: jnp.concatenate([ref[0, p] for p in range(PAIRS)], axis=1)
    if len(o_refs) > 1:
        lses = [wide(l) for l in l_refs]
        top = functools.reduce(jnp.maximum, lses)
        ws = [jnp.exp(l - top) for l in lses]
        den = functools.reduce(jnp.add, ws)
        oa = functools.reduce(jnp.add, [w * wide(o) for w, o in zip(ws, o_refs)]) / den
    else:
        oa = wide(o_refs[0])
    mix_a = _rms(oa, ga_ref[...]).astype(_bf16)
    mix_b = _rms(wide(ob_ref), gb_ref[...]).astype(_bf16)
    mix = jnp.concatenate([mix_a, mix_b], axis=1)
    return x + jnp.dot(mix, wo_ref[...], preferred_element_type=_f32)


def _ffn_slice(h, wgu_ref, wd_ref, lo, hi):
    gate = jnp.dot(h, wgu_ref[:, lo:hi], preferred_element_type=_f32)
    up = jnp.dot(h, wgu_ref[:, D_FF + lo:D_FF + hi], preferred_element_type=_f32)
    act = (gate * jax.nn.sigmoid(gate) * up).astype(_bf16)
    return jnp.dot(act, wd_ref[lo:hi, :], preferred_element_type=_f32)


def _out_ffn_kernel(*refs, n_groups):
    it = iter(refs)
    x_ref = next(it)
    o_refs = [next(it) for _ in range(n_groups)]
    l_refs = [next(it) for _ in range(n_groups)] if n_groups > 1 else []
    ob_ref, ga_ref, gb_ref, wo_ref, gf_ref, wgu_ref, wd_ref, y_ref = (next(it) for _ in range(8))
    x1 = _mix_residual(x_ref[0], o_refs, l_refs, ob_ref, ga_ref, gb_ref, wo_ref)
    h = _rms(x1, gf_ref[...]).astype(_bf16)
    y_ref[0] = x1
    for lo in range(0, D_FF, FF_CHUNK):
        y_ref[0] += _ffn_slice(h, wgu_ref, wd_ref, lo, lo + FF_CHUNK)


def _out_ffn(x, o_list, lse_list, ob, ga, gb, wo, gf, wgu, wd, tm, name):
    nb, s, _ = x.shape
    row = lambda n, j: (n, j, 0)
    act = lambda w_: pl.BlockSpec((1, tm, w_), row)
    mixer = pl.BlockSpec((1, PAIRS, tm, LANES), lambda n, j: (n, 0, j, 0))
    n_groups = len(o_list)
    in_specs = ([act(D_MODEL)] + [mixer] * (n_groups + len(lse_list) + 1)
                + [_const_spec((1, W_MIX)), _const_spec((1, W_MIX)), _const_spec((D_MODEL, D_MODEL)),
                   _const_spec((1, D_MODEL)), _const_spec((D_MODEL, 2 * D_FF)), _const_spec((D_FF, D_MODEL))])
    return pl.pallas_call(
        functools.partial(_out_ffn_kernel, n_groups=n_groups),
        grid=(nb, s // tm), in_specs=in_specs, out_specs=act(D_MODEL),
        out_shape=jax.ShapeDtypeStruct(x.shape, _f32),
        compiler_params=_params(2), name=name,
    )(x, *o_list, *lse_list, ob, ga, gb, wo, gf, wgu, wd)


def _proj_sample_kernel(x_ref, g_ref, w_ref, gqa, gka, gqb, gkb, cos_ref, sin_ref,
                        qa_ref, qb_ref, kvt_ref, scr_ref):
    qb, kb, vb = _project(x_ref[...], g_ref, w_ref, (gqa, gka, gqb, gkb),
                          cos_ref[...], sin_ref[...], scr_ref)
    qb_ref[...] = qb
    for c in range(PAIRS):
        qa_ref[:, c * LANES:(c + 1) * LANES] = scr_ref[c]
        kvt_ref[c * LANES:(c + 1) * LANES, :] = scr_ref[PAIRS + c].T
        kvt_ref[W_MIX + c * LANES:W_MIX + (c + 1) * LANES, :] = scr_ref[2 * PAIRS + c].T
    kvt_ref[2 * W_MIX:2 * W_MIX + W_KVB, :] = kb.T
    kvt_ref[2 * W_MIX + W_KVB:, :] = vb.T


def _proj_sample(x, g, w, gains, cos2, sin2, layer):
    n_tok = x.shape[0]
    full = lambda shape: pl.BlockSpec(shape, lambda i: (0,) * len(shape))
    in_specs = [full((n_tok, D_MODEL)), full((1, D_MODEL)), full((D_MODEL, COLS[-1]))] + \
               [full((1, LANES))] * 4 + [full((n_tok, LANES))] * 2
    out_shape = [jax.ShapeDtypeStruct((n_tok, W_MIX), _f32), jax.ShapeDtypeStruct((n_tok, W_MIX), _f32),
                 jax.ShapeDtypeStruct((2 * W_MIX + 2 * W_KVB, n_tok), _f32)]
    out_specs = [full((n_tok, W_MIX)), full((n_tok, W_MIX)), full((2 * W_MIX + 2 * W_KVB, n_tok))]
    return pl.pallas_call(
        _proj_sample_kernel, grid=(1,), in_specs=in_specs, out_specs=out_specs, out_shape=out_shape,
        scratch_shapes=[pltpu.VMEM((3 * PAIRS, n_tok, LANES), _f32)],
        compiler_params=_params(1), name=f"proj_sample_l{layer}",
    )(x, g, w, *gains, cos2, sin2)


def _sample_weights(t_len, la, lb):
    def mult(dist):
        m = np.zeros(dist.shape, np.float32)
        for d in DILATIONS:
            m += ((dist >= 0) & (dist % d == 0) & (dist <= BAND * d)).astype(np.float32)
        return m
    t = np.arange(t_len)[:, None]
    w_old = mult(la + t - np.arange(la)[None, :])
    u = np.arange(LANES)[None, :] - (LANES - t_len)
    d_new = np.where(u >= 0, t - u, -1)
    w_new = mult(d_new)
    d_old_b = lb + t - np.arange(lb)[None, :]
    b_old = ((d_old_b >= 0) & (d_old_b < WIN_B)).astype(np.float32)
    b_new = ((d_new >= 0) & (d_new < WIN_B)).astype(np.float32)
    def table(old, new):
        a = np.concatenate([old, new], axis=1)
        return jnp.asarray(np.concatenate([a, np.ones((Q_ROWS - t_len, a.shape[1]), np.float32)]))
    return table(w_old, w_new), table(b_old, b_new)


def _ffn_sample_kernel(x_ref, o1_ref, o4_ref, o16_ref, l1_ref, l4_ref, l16_ref, pb_ref, ga_ref, gb_ref,
                       wo_ref, gf_ref, wgu_ref, wd_ref,
                       ca_ref, cb_ref, kvt_ref, qa_ref, qb_ref, wa_ref, wb_ref, sink_ref, *rest,
                       t_len, per_tile):
    y_ref, na_ref, nb_ref, oa_ref, ob_ref, h_scr, p_scr, d_scr = rest[-8:]
    b = pl.program_id(0)
    half = pl.program_id(1)
    la = ca_ref.shape[-1]
    lo_a = la - LANES
    tail = lax.broadcasted_iota(jnp.int32, (1, LANES), 1) >= LANES - t_len
    nt = (((1,), (1,)), ((), ()))

    def new_columns():
        shift = (LANES - t_len) - t_len * (b % per_tile)
        return pltpu.roll(kvt_ref[...], jnp.where(shift < 0, shift + LANES, shift), 1)

    def shifted(old, new_rows):
        rolled = pltpu.roll(old, old.shape[-1] - t_len, 1)
        last = jnp.where(tail, new_rows, rolled[:, -LANES:])
        return rolled, last

    def shift_half(new_rows):
        rolled, last = shifted(ca_ref[...], new_rows)
        na_ref[:, 0:lo_a] = rolled[:, 0:lo_a]
        na_ref[:, lo_a:] = last

    def keys(old, new_rows):
        return jnp.concatenate([old, new_rows], axis=1).astype(_bf16)

    @pl.when(half == 0)
    def _():
        x1 = _mix_residual(x_ref[0], (o1_ref, o4_ref, o16_ref), (l1_ref, l4_ref, l16_ref), pb_ref,
                           ga_ref, gb_ref, wo_ref)
        h = _rms(x1, gf_ref[...]).astype(_bf16)
        h_scr[...] = h
        y_ref[0] = x1 + _ffn_slice(h, wgu_ref, wd_ref, 0, FF_SPLIT)

        new_t = new_columns()
        shift_half(new_t[0:W_MIX, :])
        wa = wa_ref[...]
        qa = qa_ref[0]
        for h in range(N_HEADS):
            hs = slice(h * HEAD_DIM, (h + 1) * HEAD_DIM)
            q = qa[:, hs].astype(_bf16)
            s = jnp.dot(q, keys(ca_ref[hs, :], new_t[hs, :]), preferred_element_type=_f32)
            s = jnp.where(wa > 0, s, -jnp.inf)
            m = jnp.max(s, axis=-1, keepdims=True)
            p = wa * jnp.exp(s - m)
            p_scr[h] = p
            d_scr[h] = jnp.broadcast_to(1.0 / jnp.sum(p, axis=-1, keepdims=True), (Q_ROWS, LANES))

        wb = wb_ref[...]
        qb = qb_ref[0]
        base = 2 * W_MIX
        for kv in range(2):
            _, last = shifted(cb_ref[kv], new_t[base + kv * W_KVB:base + (kv + 1) * W_KVB, :])
            nb_ref[kv] = last
        outs = []
        for h in range(N_HEADS):
            g = h // (N_HEADS // N_KV_B)
            gs = slice(g * HEAD_DIM, (g + 1) * HEAD_DIM)
            q = qb[:, h * HEAD_DIM:(h + 1) * HEAD_DIM].astype(_bf16)
            k = keys(cb_ref[0, gs, :], new_t[base + g * HEAD_DIM:base + (g + 1) * HEAD_DIM, :])
            s = jnp.dot(q, k, preferred_element_type=_f32)
            s = jnp.where(wb > 0, s, -jnp.inf)
            sink = sink_ref[h]
            m = jnp.maximum(jnp.max(s, axis=-1, keepdims=True), sink)
            p = jnp.exp(s - m)
            den = jnp.sum(p, axis=-1, keepdims=True) + jnp.exp(sink - m)
            v = keys(cb_ref[1, gs, :],
                     new_t[base + W_KVB + g * HEAD_DIM:base + W_KVB + (g + 1) * HEAD_DIM, :])
            acc = lax.dot_general(p.astype(_bf16), v, nt, preferred_element_type=_f32)
            outs.append(acc * (1.0 / den))
        ob_ref[0] = jnp.concatenate(outs, axis=1)

    @pl.when(half == 1)
    def _():
        y_ref[0] += _ffn_slice(h_scr[...], wgu_ref, wd_ref, FF_SPLIT, D_FF)

        new_t = new_columns()
        shift_half(new_t[W_MIX:2 * W_MIX, :])
        outs = []
        for h in range(N_HEADS):
            hs = slice(h * HEAD_DIM, (h + 1) * HEAD_DIM)
            v = keys(ca_ref[hs, :], new_t[W_MIX + h * HEAD_DIM:W_MIX + (h + 1) * HEAD_DIM, :])
            acc = lax.dot_general(p_scr[h].astype(_bf16), v, nt, preferred_element_type=_f32)
            outs.append(acc * d_scr[h][:, 0:1])
        oa_ref[0] = jnp.concatenate(outs, axis=1)


def _ffn_sample(x, o_list, lse_list, pb, post, cache_a, cache_b, kvt, qa, qb, weights, sinks, prev, layer,
                t_len):
    _, nbatch, _, _, la = cache_a.shape
    lb = cache_b.shape[-1]
    nb, s, _ = x.shape
    assert (nb * s) % nbatch == 0
    tm = nb * s // nbatch
    assert s % tm == 0 and tm % 16 == 0
    per_seq = s // tm
    per_tile = LANES // t_len
    row = lambda b, h: (b // per_seq, b % per_seq, 0)
    mixer = pl.BlockSpec((1, PAIRS, tm, LANES), lambda b, h: (b // per_seq, 0, b % per_seq, 0))
    tok = lambda b, h: (b, 0, 0)
    const = lambda shape: pl.BlockSpec(shape, lambda b, h: (0,) * len(shape))
    in_specs = ([pl.BlockSpec((1, tm, D_MODEL), row)] + [mixer] * 7
                + [_const_spec((1, W_MIX)), _const_spec((1, W_MIX)), _const_spec((D_MODEL, D_MODEL)),
                   _const_spec((1, D_MODEL)), _const_spec((D_MODEL, 2 * D_FF)), _const_spec((D_FF, D_MODEL))]
                + [pl.BlockSpec((None, None, None, W_MIX, la), lambda b, h: (layer, b, h, 0, 0)),
                   pl.BlockSpec((None, None, 2, W_KVB, lb), lambda b, h: (layer, b, 0, 0, 0)),
                   pl.BlockSpec((kvt.shape[0], LANES), lambda b, h: (0, b // per_tile)),
                   pl.BlockSpec((1, Q_ROWS, W_MIX), tok), pl.BlockSpec((1, Q_ROWS, W_MIX), tok),
                   const((Q_ROWS, la + LANES)), const((Q_ROWS, lb + LANES)),
                   pl.BlockSpec(memory_space=pltpu.SMEM)])
    args = [x, *o_list, *lse_list, pb, *post, cache_a, cache_b, kvt, qa, qb, *weights, sinks]
    aliases = {}
    if prev is not None:
        aliases = {len(in_specs): 1, len(in_specs) + 1: 2}
        in_specs += [pl.BlockSpec(memory_space=pl.ANY), pl.BlockSpec(memory_space=pl.ANY)]
        args += list(prev)
    out_shape = [jax.ShapeDtypeStruct(x.shape, _f32),
                 jax.ShapeDtypeStruct(cache_a.shape, _f32), jax.ShapeDtypeStruct(cache_b.shape, _f32),
                 jax.ShapeDtypeStruct((nbatch, Q_ROWS, W_MIX), _f32),
                 jax.ShapeDtypeStruct((nbatch, Q_ROWS, W_MIX), _f32)]
    out_specs = [pl.BlockSpec((1, tm, D_MODEL), row),
                 pl.BlockSpec((None, None, None, W_MIX, la), lambda b, h: (layer, b, h, 0, 0)),
                 pl.BlockSpec((None, None, 2, W_KVB, lb), lambda b, h: (layer, b, 0, 0, 0)),
                 pl.BlockSpec((1, Q_ROWS, W_MIX), tok), pl.BlockSpec((1, Q_ROWS, W_MIX), tok)]
    return pl.pallas_call(
        functools.partial(_ffn_sample_kernel, t_len=t_len, per_tile=per_tile),
        grid=(nbatch, 2), in_specs=in_specs, out_specs=out_specs, out_shape=out_shape,
        scratch_shapes=[pltpu.VMEM((tm, D_MODEL), _bf16),
                        pltpu.VMEM((N_HEADS, Q_ROWS, la + LANES), _f32),
                        pltpu.VMEM((N_HEADS, Q_ROWS, LANES), _f32)],
        input_output_aliases=aliases,
        compiler_params=_params(2), name=f"ffn_prompt_attn_sample_l{layer}",
    )(*args)


def _feature_major(c):
    l, b, rows, two, h, hd = c.shape
    return c.transpose(0, 1, 3, 4, 5, 2).reshape(l, b, two, h * hd, rows)


def _row_major(c, h):
    l, b, two, w, rows = c.shape
    return c.reshape(l, b, two, h, w // h, rows).transpose(0, 1, 5, 2, 3, 4)


def kernel(x_prompt, x_sample, cache_a, cache_b, attn_norm, w_in, q_norm_a, k_norm_a, q_norm_b,
           k_norm_b, sinks_b, out_norm_a, out_norm_b, w_out, ffn_norm, w_gate_up, w_down):
    depth = w_in.shape[0]
    nb, s_len, _ = x_prompt.shape
    dec_b, t_len, _ = x_sample.shape
    la_p, lb_p = min(BAND * DILATIONS[-1], s_len), min(WIN_B, s_len)
    assert s_len % (BAND * DILATIONS[-1]) == 0 and la_p == BAND * DILATIONS[-1] and lb_p == WIN_B
    assert LANES % t_len == 0 and (dec_b * t_len) % LANES == 0

    cos_p, sin_p = _rope_tables(jnp.arange(s_len, dtype=jnp.int32))
    pos_s = PAST_LEN + (jnp.arange(dec_b * t_len, dtype=jnp.int32) % t_len)
    cos_s, sin_s = _rope_tables(pos_s)
    ca_fm, cb_fm = _feature_major(cache_a), _feature_major(cache_b)
    la, lb = ca_fm.shape[-1], cb_fm.shape[-1]
    weights = _sample_weights(t_len, la, lb)

    shapes_p = ((depth, nb, 2, W_MIX, la_p), (depth, nb, 2, W_KVB, lb_p))
    prev_p = prev_s = None
    pad_q = lambda q: jnp.pad(q.reshape(dec_b, t_len, W_MIX), ((0, 0), (0, Q_ROWS - t_len), (0, 0)))

    xp = x_prompt
    xs = x_sample.reshape(dec_b * t_len, D_MODEL)
    two = lambda g: jnp.concatenate([g, g])[None, :]
    for layer in range(depth):
        g_attn = attn_norm[layer][None, :]
        w = w_in[layer].astype(_bf16)
        gains = (two(q_norm_a[layer]), two(k_norm_a[layer]), two(q_norm_b[layer]), two(k_norm_b[layer]))
        post = (out_norm_a[layer][None, :], out_norm_b[layer][None, :], w_out[layer].astype(_bf16),
                ffn_norm[layer][None, :], w_gate_up[layer].astype(_bf16), w_down[layer].astype(_bf16))
        sinks = sinks_b[layer]

        (q1, k1, v1, q4, k4, v4, q16, k16, v16, qb, kb, vb, new_a_p, new_b_p) = _proj_prompt(
            xp, g_attn, w, gains, cos_p, sin_p, shapes_p, prev_p, layer)
        prev_p = (new_a_p, new_b_p)
        o_list, lse_list = [], []
        for d, (q, k, v) in zip(DILATIONS, ((q1, k1, v1), (q4, k4, v4), (q16, k16, v16))):
            o, lse = _band_attn(q, k, v, d, max(1024, BAND * d), BAND, layer=layer)
            o_list.append(o)
            lse_list.append(lse)
        (ob,) = _band_attn(qb[:, None], kb[:, None], vb[:, None], 1, 1024, WIN_B - 1, sinks=sinks,
                           layer=layer)
        qa_s, qb_s, kvt = _proj_sample(xs, g_attn, w, gains, cos_s, sin_s, layer)
        xp, new_a_s, new_b_s, oa_s, ob_s = _ffn_sample(
            xp, o_list, lse_list, ob, post, ca_fm, cb_fm, kvt, pad_q(qa_s), pad_q(qb_s), weights, sinks,
            prev_s, layer, t_len)
        prev_s = (new_a_s, new_b_s)
        unpad = lambda o: o[:, :t_len].reshape(dec_b * t_len, PAIRS, LANES).transpose(1, 0, 2)[None]
        xs = _out_ffn(xs[None], [unpad(oa_s)], [], unpad(ob_s), *post, tm=dec_b * t_len,
                      name=f"out_ffn_sample_l{layer}")[0]

    return (xp, xs.reshape(dec_b, t_len, D_MODEL), _row_major(new_a_p, N_HEADS), _row_major(new_b_p, N_KV_B),
            _row_major(new_a_s, N_HEADS), _row_major(new_b_s, N_KV_B))
```

```python
import functools
import math

import numpy as np
import jax
import jax.numpy as jnp
from jax import lax
from jax.experimental import pallas as pl
from jax.experimental.pallas import tpu as pltpu

D_MODEL = 1024
HEAD_DIM = 64
N_HEADS = 8
N_KV_B = 2
W_MIX = N_HEADS * HEAD_DIM
W_KVB = N_KV_B * HEAD_DIM
D_FF = 2816
DILATIONS = (1, 4, 16)
BAND = 128
WIN_B = 128
ROT_DIM = 16
ROPE_THETA = 500000.0
EPS = 1e-6
SCALE = HEAD_DIM ** -0.5
PAST_LEN = 16384
LANES = 128
COLS = (0, 512, 1024, 1536, 2048, 2176, 2304)
VMEM_LIMIT = 60 * 1024 * 1024
FF_CHUNK = D_FF // 2
FF_SPLIT = 768
PAIRS = W_MIX // LANES
Q_ROWS = 8

_bf16 = jnp.bfloat16
_f32 = jnp.float32


def _params(n_axes):
    return pltpu.CompilerParams(dimension_semantics=("arbitrary",) * n_axes,
                                vmem_limit_bytes=VMEM_LIMIT)


def _const_spec(shape):
    nd = len(shape)
    return pl.BlockSpec(shape, lambda *_: (0,) * nd, pipeline_mode=pl.Buffered(1))


def _lane_lt64(shape):
    return lax.broadcasted_iota(jnp.int32, shape, len(shape) - 1) % LANES < HEAD_DIM


def _rms(x, g):
    return x * lax.rsqrt(jnp.mean(x * x, axis=-1, keepdims=True) + EPS) * g


def _head_norm(y, g2):
    lo = _lane_lt64(y.shape)
    sq = y * y
    s_lo = jnp.sum(jnp.where(lo, sq, 0.0), axis=-1, keepdims=True)
    s_hi = jnp.sum(jnp.where(lo, 0.0, sq), axis=-1, keepdims=True)
    ms = jnp.where(lo, s_lo, s_hi) * (1.0 / HEAD_DIM)
    return y * lax.rsqrt(ms + EPS) * g2


def _rope(y, cos2, sin2):
    lane = lax.broadcasted_iota(jnp.int32, y.shape, 1)
    swap = jnp.where(lane % HEAD_DIM < ROT_DIM, lane ^ (ROT_DIM // 2), lane)
    partner = jnp.take_along_axis(y, swap, axis=1)
    return y * cos2 + partner * sin2


def _rope_tables(pos):
    half = ROT_DIM // 2
    inv = jnp.exp(-math.log(ROPE_THETA) * jnp.arange(0, ROT_DIM, 2, dtype=_f32) / ROT_DIM)
    ang = pos.astype(_f32)[:, None] * inv[None, :]
    cos, sin = jnp.cos(ang), jnp.sin(ang)
    ones = jnp.ones((pos.shape[0], HEAD_DIM - ROT_DIM), _f32)
    c = jnp.concatenate([cos, cos, ones], axis=1)
    s = jnp.concatenate([-sin, sin, 0.0 * ones], axis=1)
    return jnp.concatenate([c, c], axis=1), jnp.concatenate([s, s], axis=1)


def _project(x, g_ref, w_ref, gains, cos2, sin2, scr_ref):
    gqa, gka, gqb, gkb = gains
    h = _rms(x, g_ref[...]).astype(_bf16)

    def seg(i):
        return jnp.dot(h, w_ref[:, COLS[i]:COLS[i + 1]], preferred_element_type=_f32)

    def norm_rope(y, g2, scale):
        outs = []
        for c in range(y.shape[1] // LANES):
            t = _rope(_head_norm(y[:, c * LANES:(c + 1) * LANES], g2), cos2, sin2)
            outs.append(t * scale if scale != 1.0 else t)
        return outs

    segs = [seg(i) for i in range(len(COLS) - 1)]
    for c, t in enumerate(norm_rope(segs[0], gqa[...], SCALE)):
        scr_ref[c] = t
    for c, t in enumerate(norm_rope(segs[1], gka[...], 1.0)):
        scr_ref[PAIRS + c] = t
    for c in range(PAIRS):
        scr_ref[2 * PAIRS + c] = segs[2][:, c * LANES:(c + 1) * LANES]
    qb = jnp.concatenate(norm_rope(segs[3], gqb[...], SCALE), axis=1)
    kb = norm_rope(segs[4], gkb[...], 1.0)[0]
    return qb, kb, segs[5]


def _dup_heads(y):
    lo = _lane_lt64(y.shape)
    sw = pltpu.roll(y, HEAD_DIM, 1)
    return jnp.concatenate([jnp.where(lo, y, sw), jnp.where(lo, sw, y)], axis=1)


def _proj_prompt_kernel(x_ref, g_ref, w_ref, gqa, gka, gqb, gkb, cos_ref, sin_ref,
                        *rest, tm, n_blocks, ca_blocks):
    (q1, k1, v1, q4, k4, v4, q16, k16, v16, qb_ref, kb_ref, vb_ref, ca_ref, cb_ref, scr_ref) = rest[-15:]
    j = pl.program_id(1)
    qb, kb, vb = _project(x_ref[0], g_ref, w_ref, (gqa, gka, gqb, gkb),
                          cos_ref[...], sin_ref[...], scr_ref)
    qb_ref[0] = qb.astype(_bf16)
    kb_ref[0] = _dup_heads(kb).astype(_bf16)
    vb_ref[0] = _dup_heads(vb).astype(_bf16)
    for d, (qo, ko, vo) in zip(DILATIONS, ((q1, k1, v1), (q4, k4, v4), (q16, k16, v16))):
        for r in range(d):
            rows = pl.ds(r, tm // d, stride=d) if d > 1 else slice(None)
            for i, out in enumerate((qo, ko, vo)):
                for c in range(PAIRS):
                    out[0, r, :, c * LANES:(c + 1) * LANES] = scr_ref[i * PAIRS + c, rows, :].astype(_bf16)

    @pl.when(j >= n_blocks - ca_blocks)
    def _():
        for kv in range(2):
            for c in range(PAIRS):
                ca_ref[0, kv, c * LANES:(c + 1) * LANES, :] = scr_ref[(1 + kv) * PAIRS + c].T

    @pl.when(j == n_blocks - 1)
    def _():
        cb_ref[0, 0] = kb[tm - WIN_B:, :].T
        cb_ref[0, 1] = vb[tm - WIN_B:, :].T


def _proj_prompt(x, g, w, gains, cos2, sin2, cache_shapes, prev, layer, tm=512):
    nb, s, _ = x.shape
    n_blocks = s // tm
    ca_blocks = cache_shapes[0][-1] // tm
    row = lambda n, j: (n, j, 0)
    lay = lambda n, j: (n, 0, j, 0)
    vec = lambda w_: _const_spec((1, w_))
    in_specs = [pl.BlockSpec((1, tm, D_MODEL), row), vec(D_MODEL), _const_spec((D_MODEL, COLS[-1])),
                vec(LANES), vec(LANES), vec(LANES), vec(LANES),
                pl.BlockSpec((tm, LANES), lambda n, j: (j, 0)), pl.BlockSpec((tm, LANES), lambda n, j: (j, 0))]
    args = [x, g, w, *gains, cos2, sin2]
    aliases = {}
    if prev is not None:
        aliases = {len(in_specs): 12, len(in_specs) + 1: 13}
        in_specs += [pl.BlockSpec(memory_space=pl.ANY), pl.BlockSpec(memory_space=pl.ANY)]
        args += list(prev)
    out_shape, out_specs = [], []
    for d in DILATIONS:
        for _ in range(3):
            out_shape.append(jax.ShapeDtypeStruct((nb, d, s // d, W_MIX), _bf16))
            out_specs.append(pl.BlockSpec((1, d, tm // d, W_MIX), lay))
    out_shape += [jax.ShapeDtypeStruct((nb, s, W_MIX), _bf16),
                  jax.ShapeDtypeStruct((nb, s, 2 * W_KVB), _bf16),
                  jax.ShapeDtypeStruct((nb, s, 2 * W_KVB), _bf16),
                  jax.ShapeDtypeStruct(cache_shapes[0], _f32),
                  jax.ShapeDtypeStruct(cache_shapes[1], _f32)]
    first = n_blocks - ca_blocks
    out_specs += [pl.BlockSpec((1, tm, W_MIX), row),
                  pl.BlockSpec((1, tm, 2 * W_KVB), row),
                  pl.BlockSpec((1, tm, 2 * W_KVB), row),
                  pl.BlockSpec((None, 1, 2, W_MIX, tm),
                               lambda n, j: (layer, n, 0, 0, jnp.maximum(j - first, 0))),
                  pl.BlockSpec((None, 1, 2, W_KVB, WIN_B), lambda n, j: (layer, n, 0, 0, 0))]
    return pl.pallas_call(
        functools.partial(_proj_prompt_kernel, tm=tm, n_blocks=n_blocks, ca_blocks=ca_blocks),
        grid=(nb, n_blocks), in_specs=in_specs, out_specs=out_specs, out_shape=out_shape,
        scratch_shapes=[pltpu.VMEM((3 * PAIRS, tm, LANES), _f32)],
        input_output_aliases=aliases,
        compiler_params=_params(2), name=f"proj_prompt_l{layer}",
    )(*args)


def _band_attn_kernel(*refs, d, n_sub, max_dist, swa, with_lse):
    it = iter(refs)
    q_ref, kc_ref, kp_ref, vc_ref, vp_ref = (next(it) for _ in range(5))
    sink_ref = next(it) if swa else None
    o_ref = next(it)
    lse_ref = next(it) if with_lse else None
    kcat, vcat = next(it), next(it)
    c = pl.program_id(1)
    kcat[:, 0:BAND, :] = kp_ref[0]
    kcat[:, BAND:, :] = kc_ref[0]
    vcat[:, 0:BAND, :] = vp_ref[0]
    vcat[:, BAND:, :] = vc_ref[0]

    qi = lax.broadcasted_iota(jnp.int32, (2 * BAND, BAND), 0) % BAND
    kj = lax.broadcasted_iota(jnp.int32, (2 * BAND, BAND), 1)
    from_prev = kj > qi
    diag = kj == qi
    has_diag = max_dist == BAND
    lo_q = _lane_lt64((BAND, LANES))
    top = lax.broadcasted_iota(jnp.int32, (2 * BAND, 1), 0) < BAND

    kcols = [(p // 2) * LANES if swa else p * LANES for p in range(PAIRS)]
    ones = jnp.ones((2 * BAND, LANES), _bf16)

    def body(idx, carry):
        r = idx // n_sub
        sb = idx % n_sub
        row0 = pl.multiple_of(sb * BAND, BAND)
        prev_bias = jnp.where((c == 0) & (sb == 0), -jnp.inf, 0.0).astype(_f32)
        q_blk = q_ref[0, r, pl.ds(row0, BAND), :]

        def score_dot(p):
            q2 = q_blk[:, p * LANES:(p + 1) * LANES]
            zero = jnp.zeros_like(q2)
            qs = jnp.concatenate([jnp.where(lo_q, q2, zero), jnp.where(lo_q, zero, q2)], axis=0)
            k2 = kcat[r, pl.ds(row0, 2 * BAND), kcols[p]:kcols[p] + LANES]
            return lax.dot_general(qs, k2, (((1,), (1,)), ((), ())), preferred_element_type=_f32)

        def softmax(p, s2):
            s_prev = s2[:, :BAND] + prev_bias
            s = jnp.where(from_prev, s_prev, s2[:, BAND:])
            m = jnp.max(s, axis=-1, keepdims=True)
            if has_diag:
                s_far = jnp.sum(jnp.where(diag, s_prev, 0.0), axis=-1, keepdims=True)
                m = jnp.maximum(m, s_far)
            if swa:
                m = jnp.maximum(m, jnp.where(top, sink_ref[2 * p], sink_ref[2 * p + 1]))
            pe = jnp.exp((s - m).astype(_bf16))
            zero = jnp.zeros_like(pe)
            p_prev = jnp.where(from_prev, pe, zero)
            if has_diag:
                p_prev = jnp.where(diag, jnp.exp((s_far - m).astype(_bf16)), p_prev)
            p_cur = jnp.where(from_prev, zero, pe)
            return jnp.concatenate([p_prev, p_cur], axis=1), m

        def value_dot(p, probs):
            v2 = vcat[r, pl.ds(row0, 2 * BAND), kcols[p]:kcols[p] + LANES]
            return jnp.dot(probs, jnp.concatenate([v2, ones], axis=1), preferred_element_type=_f32)

        def finish(p, both, m):
            den = both[:, LANES:]
            if swa:
                den = den + jnp.exp(jnp.where(top, sink_ref[2 * p], sink_ref[2 * p + 1]) - m)
            acc = both[:, :LANES] * (1.0 / den)
            out_rows = (pl.ds(r + d * row0, BAND, stride=d) if d > 1 else pl.ds(row0, BAND))
            o_ref[0, p, out_rows, :] = jnp.where(lo_q, acc[:BAND], acc[BAND:])
            if with_lse:
                lse = m + jnp.log(den)
                lse_ref[0, p, out_rows, :] = jnp.where(lo_q, lse[:BAND], lse[BAND:])

        scores = [score_dot(p) for p in range(PAIRS)]
        soft = [softmax(p, scores[p]) for p in range(PAIRS)]
        boths = [value_dot(p, soft[p][0]) for p in range(PAIRS)]
        for p in range(PAIRS):
            finish(p, boths[p], soft[p][1])
        return carry

    lax.fori_loop(0, d * n_sub, body, 0)


def _band_attn(q, k, v, d, chunk, max_dist, sinks=None, layer=0):
    assert max_dist in (BAND - 1, BAND)
    nb, _, ls, wk = k.shape
    s = ls * d
    rows = chunk // d
    n_sub = rows // BAND
    swa = sinks is not None
    cur = lambda n, c: (n, 0, c, 0)
    prev = lambda n, c: (n, 0, jnp.maximum(c * n_sub - 1, 0), 0)
    in_specs = [pl.BlockSpec((1, d, rows, W_MIX), cur),
                pl.BlockSpec((1, d, rows, wk), cur), pl.BlockSpec((1, d, BAND, wk), prev),
                pl.BlockSpec((1, d, rows, wk), cur), pl.BlockSpec((1, d, BAND, wk), prev)]
    args = [q, k, k, v, v]
    if swa:
        in_specs.append(pl.BlockSpec(memory_space=pltpu.SMEM))
        args.append(sinks)
    nat = pl.BlockSpec((1, PAIRS, chunk, LANES), lambda n, c: (n, 0, c, 0))
    out_shape = [jax.ShapeDtypeStruct((nb, PAIRS, s, LANES), _f32)]
    out_specs = [nat]
    if not swa:
        out_shape.append(jax.ShapeDtypeStruct((nb, PAIRS, s, LANES), _f32))
        out_specs.append(nat)
    return pl.pallas_call(
        functools.partial(_band_attn_kernel, d=d, n_sub=n_sub, max_dist=max_dist, swa=swa,
                          with_lse=not swa),
        grid=(nb, s // chunk), in_specs=in_specs, out_specs=out_specs, out_shape=out_shape,
        scratch_shapes=[pltpu.VMEM((d, rows + BAND, wk), _bf16), pltpu.VMEM((d, rows + BAND, wk), _bf16)],
        compiler_params=_params(2), name=f"attn_d{d}_{'swa' if swa else 'dil'}_l{layer}",
    )(*args)


def _mix_residual(x, o_refs, l_refs, ob_ref, ga_ref, gb_ref, wo_ref):
    wide = lambda ref: jnp.concatenate([ref[0, p] for p in range(PAIRS)], axis=1)
    if len(o_refs) > 1:
        lses = [wide(l) for l in l_refs]
        top = functools.reduce(jnp.maximum, lses)
        ws = [jnp.exp(l - top) for l in lses]
        den = functools.reduce(jnp.add, ws)
        oa = functools.reduce(jnp.add, [w * wide(o) for w, o in zip(ws, o_refs)]) / den
    else:
        oa = wide(o_refs[0])
    mix_a = _rms(oa, ga_ref[...]).astype(_bf16)
    mix_b = _rms(wide(ob_ref), gb_ref[...]).astype(_bf16)
    mix = jnp.concatenate([mix_a, mix_b], axis=1)
    return x + jnp.dot(mix, wo_ref[...], preferred_element_type=_f32)


def _ffn_slice(h, wgu_ref, wd_ref, lo, hi):
    gate = jnp.dot(h, wgu_ref[:, lo:hi], preferred_element_type=_f32)
    up = jnp.dot(h, wgu_ref[:, D_FF + lo:D_FF + hi], preferred_element_type=_f32)
    act = (gate * jax.nn.sigmoid(gate) * up).astype(_bf16)
    return jnp.dot(act, wd_ref[lo:hi, :], preferred_element_type=_f32)


def _out_ffn_kernel(*refs, n_groups):
    it = iter(refs)
    x_ref = next(it)
    o_refs = [next(it) for _ in range(n_groups)]
    l_refs = [next(it) for _ in range(n_groups)] if n_groups > 1 else []
    ob_ref, ga_ref, gb_ref, wo_ref, gf_ref, wgu_ref, wd_ref, y_ref = (next(it) for _ in range(8))
    x1 = _mix_residual(x_ref[0], o_refs, l_refs, ob_ref, ga_ref, gb_ref, wo_ref)
    h = _rms(x1, gf_ref[...]).astype(_bf16)
    y_ref[0] = x1
    for lo in range(0, D_FF, FF_CHUNK):
        y_ref[0] += _ffn_slice(h, wgu_ref, wd_ref, lo, lo + FF_CHUNK)


def _out_ffn(x, o_list, lse_list, ob, ga, gb, wo, gf, wgu, wd, tm, name):
    nb, s, _ = x.shape
    row = lambda n, j: (n, j, 0)
    act = lambda w_: pl.BlockSpec((1, tm, w_), row)
    mixer = pl.BlockSpec((1, PAIRS, tm, LANES), lambda n, j: (n, 0, j, 0))
    n_groups = len(o_list)
    in_specs = ([act(D_MODEL)] + [mixer] * (n_groups + len(lse_list) + 1)
                + [_const_spec((1, W_MIX)), _const_spec((1, W_MIX)), _const_spec((D_MODEL, D_MODEL)),
                   _const_spec((1, D_MODEL)), _const_spec((D_MODEL, 2 * D_FF)), _const_spec((D_FF, D_MODEL))])
    return pl.pallas_call(
        functools.partial(_out_ffn_kernel, n_groups=n_groups),
        grid=(nb, s // tm), in_specs=in_specs, out_specs=act(D_MODEL),
        out_shape=jax.ShapeDtypeStruct(x.shape, _f32),
        compiler_params=_params(2), name=name,
    )(x, *o_list, *lse_list, ob, ga, gb, wo, gf, wgu, wd)


def _proj_sample_kernel(x_ref, g_ref, w_ref, gqa, gka, gqb, gkb, cos_ref, sin_ref,
                        qa_ref, qb_ref, kvt_ref, scr_ref):
    qb, kb, vb = _project(x_ref[...], g_ref, w_ref, (gqa, gka, gqb, gkb),
                          cos_ref[...], sin_ref[...], scr_ref)
    qb_ref[...] = qb
    for c in range(PAIRS):
        qa_ref[:, c * LANES:(c + 1) * LANES] = scr_ref[c]
        kvt_ref[c * LANES:(c + 1) * LANES, :] = scr_ref[PAIRS + c].T
        kvt_ref[W_MIX + c * LANES:W_MIX + (c + 1) * LANES, :] = scr_ref[2 * PAIRS + c].T
    kvt_ref[2 * W_MIX:2 * W_MIX + W_KVB, :] = kb.T
    kvt_ref[2 * W_MIX + W_KVB:, :] = vb.T


def _proj_sample(x, g, w, gains, cos2, sin2, layer):
    n_tok = x.shape[0]
    full = lambda shape: pl.BlockSpec(shape, lambda i: (0,) * len(shape))
    in_specs = [full((n_tok, D_MODEL)), full((1, D_MODEL)), full((D_MODEL, COLS[-1]))] + \
               [full((1, LANES))] * 4 + [full((n_tok, LANES))] * 2
    out_shape = [jax.ShapeDtypeStruct((n_tok, W_MIX), _f32), jax.ShapeDtypeStruct((n_tok, W_MIX), _f32),
                 jax.ShapeDtypeStruct((2 * W_MIX + 2 * W_KVB, n_tok), _f32)]
    out_specs = [full((n_tok, W_MIX)), full((n_tok, W_MIX)), full((2 * W_MIX + 2 * W_KVB, n_tok))]
    return pl.pallas_call(
        _proj_sample_kernel, grid=(1,), in_specs=in_specs, out_specs=out_specs, out_shape=out_shape,
        scratch_shapes=[pltpu.VMEM((3 * PAIRS, n_tok, LANES), _f32)],
        compiler_params=_params(1), name=f"proj_sample_l{layer}",
    )(x, g, w, *gains, cos2, sin2)


def _sample_weights(t_len, la, lb):
    def mult(dist):
        m = np.zeros(dist.shape, np.float32)
        for d in DILATIONS:
            m += ((dist >= 0) & (dist % d == 0) & (dist <= BAND * d)).astype(np.float32)
        return m
    t = np.arange(t_len)[:, None]
    w_old = mult(la + t - np.arange(la)[None, :])
    u = np.arange(LANES)[None, :] - (LANES - t_len)
    d_new = np.where(u >= 0, t - u, -1)
    w_new = mult(d_new)
    d_old_b = lb + t - np.arange(lb)[None, :]
    b_old = ((d_old_b >= 0) & (d_old_b < WIN_B)).astype(np.float32)
    b_new = ((d_new >= 0) & (d_new < WIN_B)).astype(np.float32)
    def table(old, new):
        a = np.concatenate([old, new], axis=1)
        return jnp.asarray(np.concatenate([a, np.ones((Q_ROWS - t_len, a.shape[1]), np.float32)]))
    return table(w_old, w_new), table(b_old, b_new)


def _ffn_sample_kernel(x_ref, o1_ref, o4_ref, o16_ref, l1_ref, l4_ref, l16_ref, pb_ref, ga_ref, gb_ref,
                       wo_ref, gf_ref, wgu_ref, wd_ref,
                       ca_ref, cb_ref, kvt_ref, qa_ref, qb_ref, wa_ref, wb_ref, sink_ref, *rest,
                       t_len, per_tile):
    y_ref, na_ref, nb_ref, oa_ref, ob_ref, h_scr, p_scr, d_scr = rest[-8:]
    b = pl.program_id(0)
    half = pl.program_id(1)
    la = ca_ref.shape[-1]
    lo_a = la - LANES
    tail = lax.broadcasted_iota(jnp.int32, (1, LANES), 1) >= LANES - t_len
    nt = (((1,), (1,)), ((), ()))

    def new_columns():
        shift = (LANES - t_len) - t_len * (b % per_tile)
        return pltpu.roll(kvt_ref[...], jnp.where(shift < 0, shift + LANES, shift), 1)

    def shifted(old, new_rows):
        rolled = pltpu.roll(old, old.shape[-1] - t_len, 1)
        last = jnp.where(tail, new_rows, rolled[:, -LANES:])
        return rolled, last

    def shift_half(new_rows):
        rolled, last = shifted(ca_ref[...], new_rows)
        na_ref[:, 0:lo_a] = rolled[:, 0:lo_a]
        na_ref[:, lo_a:] = last

    def keys(old, new_rows):
        return jnp.concatenate([old, new_rows], axis=1).astype(_bf16)

    @pl.when(half == 0)
    def _():
        x1 = _mix_residual(x_ref[0], (o1_ref, o4_ref, o16_ref), (l1_ref, l4_ref, l16_ref), pb_ref,
                           ga_ref, gb_ref, wo_ref)
        h = _rms(x1, gf_ref[...]).astype(_bf16)
        h_scr[...] = h
        y_ref[0] = x1 + _ffn_slice(h, wgu_ref, wd_ref, 0, FF_SPLIT)

        new_t = new_columns()
        shift_half(new_t[0:W_MIX, :])
        wa = wa_ref[...]
        qa = qa_ref[0]
        for h in range(N_HEADS):
            hs = slice(h * HEAD_DIM, (h + 1) * HEAD_DIM)
            q = qa[:, hs].astype(_bf16)
            s = jnp.dot(q, keys(ca_ref[hs, :], new_t[hs, :]), preferred_element_type=_f32)
            s = jnp.where(wa > 0, s, -jnp.inf)
            m = jnp.max(s, axis=-1, keepdims=True)
            p = wa * jnp.exp(s - m)
            p_scr[h] = p
            d_scr[h] = jnp.broadcast_to(1.0 / jnp.sum(p, axis=-1, keepdims=True), (Q_ROWS, LANES))

        wb = wb_ref[...]
        qb = qb_ref[0]
        base = 2 * W_MIX
        for kv in range(2):
            _, last = shifted(cb_ref[kv], new_t[base + kv * W_KVB:base + (kv + 1) * W_KVB, :])
            nb_ref[kv] = last
        outs = []
        for h in range(N_HEADS):
            g = h // (N_HEADS // N_KV_B)
            gs = slice(g * HEAD_DIM, (g + 1) * HEAD_DIM)
            q = qb[:, h * HEAD_DIM:(h + 1) * HEAD_DIM].astype(_bf16)
            k = keys(cb_ref[0, gs, :], new_t[base + g * HEAD_DIM:base + (g + 1) * HEAD_DIM, :])
            s = jnp.dot(q, k, preferred_element_type=_f32)
            s = jnp.where(wb > 0, s, -jnp.inf)
            sink = sink_ref[h]
            m = jnp.maximum(jnp.max(s, axis=-1, keepdims=True), sink)
            p = jnp.exp(s - m)
            den = jnp.sum(p, axis=-1, keepdims=True) + jnp.exp(sink - m)
            v = keys(cb_ref[1, gs, :],
                     new_t[base + W_KVB + g * HEAD_DIM:base + W_KVB + (g + 1) * HEAD_DIM, :])
            acc = lax.dot_general(p.astype(_bf16), v, nt, preferred_element_type=_f32)
            outs.append(acc * (1.0 / den))
        ob_ref[0] = jnp.concatenate(outs, axis=1)

    @pl.when(half == 1)
    def _():
        y_ref[0] += _ffn_slice(h_scr[...], wgu_ref, wd_ref, FF_SPLIT, D_FF)

        new_t = new_columns()
        shift_half(new_t[W_MIX:2 * W_MIX, :])
        outs = []
        for h in range(N_HEADS):
            hs = slice(h * HEAD_DIM, (h + 1) * HEAD_DIM)
            v = keys(ca_ref[hs, :], new_t[W_MIX + h * HEAD_DIM:W_MIX + (h + 1) * HEAD_DIM, :])
            acc = lax.dot_general(p_scr[h].astype(_bf16), v, nt, preferred_element_type=_f32)
            outs.append(acc * d_scr[h][:, 0:1])
        oa_ref[0] = jnp.concatenate(outs, axis=1)


def _ffn_sample(x, o_list, lse_list, pb, post, cache_a, cache_b, kvt, qa, qb, weights, sinks, prev, layer,
                t_len):
    _, nbatch, _, _, la = cache_a.shape
    lb = cache_b.shape[-1]
    nb, s, _ = x.shape
    assert (nb * s) % nbatch == 0
    tm = nb * s // nbatch
    assert s % tm == 0 and tm % 16 == 0
    per_seq = s // tm
    per_tile = LANES // t_len
    row = lambda b, h: (b // per_seq, b % per_seq, 0)
    mixer = pl.BlockSpec((1, PAIRS, tm, LANES), lambda b, h: (b // per_seq, 0, b % per_seq, 0))
    tok = lambda b, h: (b, 0, 0)
    const = lambda shape: pl.BlockSpec(shape, lambda b, h: (0,) * len(shape))
    in_specs = ([pl.BlockSpec((1, tm, D_MODEL), row)] + [mixer] * 7
                + [_const_spec((1, W_MIX)), _const_spec((1, W_MIX)), _const_spec((D_MODEL, D_MODEL)),
                   _const_spec((1, D_MODEL)), _const_spec((D_MODEL, 2 * D_FF)), _const_spec((D_FF, D_MODEL))]
                + [pl.BlockSpec((None, None, None, W_MIX, la), lambda b, h: (layer, b, h, 0, 0)),
                   pl.BlockSpec((None, None, 2, W_KVB, lb), lambda b, h: (layer, b, 0, 0, 0)),
                   pl.BlockSpec((kvt.shape[0], LANES), lambda b, h: (0, b // per_tile)),
                   pl.BlockSpec((1, Q_ROWS, W_MIX), tok), pl.BlockSpec((1, Q_ROWS, W_MIX), tok),
                   const((Q_ROWS, la + LANES)), const((Q_ROWS, lb + LANES)),
                   pl.BlockSpec(memory_space=pltpu.SMEM)])
    args = [x, *o_list, *lse_list, pb, *post, cache_a, cache_b, kvt, qa, qb, *weights, sinks]
    aliases = {}
    if prev is not None:
        aliases = {len(in_specs): 1, len(in_specs) + 1: 2}
        in_specs += [pl.BlockSpec(memory_space=pl.ANY), pl.BlockSpec(memory_space=pl.ANY)]
        args += list(prev)
    out_shape = [jax.ShapeDtypeStruct(x.shape, _f32),
                 jax.ShapeDtypeStruct(cache_a.shape, _f32), jax.ShapeDtypeStruct(cache_b.shape, _f32),
                 jax.ShapeDtypeStruct((nbatch, Q_ROWS, W_MIX), _f32),
                 jax.ShapeDtypeStruct((nbatch, Q_ROWS, W_MIX), _f32)]
    out_specs = [pl.BlockSpec((1, tm, D_MODEL), row),
                 pl.BlockSpec((None, None, None, W_MIX, la), lambda b, h: (layer, b, h, 0, 0)),
                 pl.BlockSpec((None, None, 2, W_KVB, lb), lambda b, h: (layer, b, 0, 0, 0)),
                 pl.BlockSpec((1, Q_ROWS, W_MIX), tok), pl.BlockSpec((1, Q_ROWS, W_MIX), tok)]
    return pl.pallas_call(
        functools.partial(_ffn_sample_kernel, t_len=t_len, per_tile=per_tile),
        grid=(nbatch, 2), in_specs=in_specs, out_specs=out_specs, out_shape=out_shape,
        scratch_shapes=[pltpu.VMEM((tm, D_MODEL), _bf16),
                        pltpu.VMEM((N_HEADS, Q_ROWS, la + LANES), _f32),
                        pltpu.VMEM((N_HEADS, Q_ROWS, LANES), _f32)],
        input_output_aliases=aliases,
        compiler_params=_params(2), name=f"ffn_prompt_attn_sample_l{layer}",
    )(*args)


def _feature_major(c):
    l, b, rows, two, h, hd = c.shape
    return c.transpose(0, 1, 3, 4, 5, 2).reshape(l, b, two, h * hd, rows)


def _row_major(c, h):
    l, b, two, w, rows = c.shape
    return c.reshape(l, b, two, h, w // h, rows).transpose(0, 1, 5, 2, 3, 4)


def kernel(x_prompt, x_sample, cache_a, cache_b, attn_norm, w_in, q_norm_a, k_norm_a, q_norm_b,
           k_norm_b, sinks_b, out_norm_a, out_norm_b, w_out, ffn_norm, w_gate_up, w_down):
    depth = w_in.shape[0]
    nb, s_len, _ = x_prompt.shape
    dec_b, t_len, _ = x_sample.shape
    la_p, lb_p = min(BAND * DILATIONS[-1], s_len), min(WIN_B, s_len)
    assert s_len % (BAND * DILATIONS[-1]) == 0 and la_p == BAND * DILATIONS[-1] and lb_p == WIN_B
    assert LANES % t_len == 0 and (dec_b * t_len) % LANES == 0

    cos_p, sin_p = _rope_tables(jnp.arange(s_len, dtype=jnp.int32))
    pos_s = PAST_LEN + (jnp.arange(dec_b * t_len, dtype=jnp.int32) % t_len)
    cos_s, sin_s = _rope_tables(pos_s)
    ca_fm, cb_fm = _feature_major(cache_a), _feature_major(cache_b)
    la, lb = ca_fm.shape[-1], cb_fm.shape[-1]
    weights = _sample_weights(t_len, la, lb)

    shapes_p = ((depth, nb, 2, W_MIX, la_p), (depth, nb, 2, W_KVB, lb_p))
    prev_p = prev_s = None
    pad_q = lambda q: jnp.pad(q.reshape(dec_b, t_len, W_MIX), ((0, 0), (0, Q_ROWS - t_len), (0, 0)))

    xp = x_prompt
    xs = x_sample.reshape(dec_b * t_len, D_MODEL)
    two = lambda g: jnp.concatenate([g, g])[None, :]
    for layer in range(depth):
        g_attn = attn_norm[layer][None, :]
        w = w_in[layer].astype(_bf16)
        gains = (two(q_norm_a[layer]), two(k_norm_a[layer]), two(q_norm_b[layer]), two(k_norm_b[layer]))
        post = (out_norm_a[layer][None, :], out_norm_b[layer][None, :], w_out[layer].astype(_bf16),
                ffn_norm[layer][None, :], w_gate_up[layer].astype(_bf16), w_down[layer].astype(_bf16))
        sinks = sinks_b[layer]

        (q1, k1, v1, q4, k4, v4, q16, k16, v16, qb, kb, vb, new_a_p, new_b_p) = _proj_prompt(
            xp, g_attn, w, gains, cos_p, sin_p, shapes_p, prev_p, layer)
        prev_p = (new_a_p, new_b_p)
        o_list, lse_list = [], []
        for d, (q, k, v) in zip(DILATIONS, ((q1, k1, v1), (q4, k4, v4), (q16, k16, v16))):
            o, lse = _band_attn(q, k, v, d, max(1024, BAND * d), BAND, layer=layer)
            o_list.append(o)
            lse_list.append(lse)
        (ob,) = _band_attn(qb[:, None], kb[:, None], vb[:, None], 1, 1024, WIN_B - 1, sinks=sinks,
                           layer=layer)
        qa_s, qb_s, kvt = _proj_sample(xs, g_attn, w, gains, cos_s, sin_s, layer)
        xp, new_a_s, new_b_s, oa_s, ob_s = _ffn_sample(
            xp, o_list, lse_list, ob, post, ca_fm, cb_fm, kvt, pad_q(qa_s), pad_q(qb_s), weights, sinks,
            prev_s, layer, t_len)
        prev_s = (new_a_s, new_b_s)
        unpad = lambda o: o[:, :t_len].reshape(dec_b * t_len, PAIRS, LANES).transpose(1, 0, 2)[None]
        xs = _out_ffn(xs[None], [unpad(oa_s)], [], unpad(ob_s), *post, tm=dec_b * t_len,
                      name=f"out_ffn_sample_l{layer}")[0]

    return (xp, xs.reshape(dec_b, t_len, D_MODEL), _row_major(new_a_p, N_HEADS), _row_major(new_b_p, N_KV_B),
            _row_major(new_a_s, N_HEADS), _row_major(new_b_s, N_KV_B))
```

```python
import functools
import math

import numpy as np
import jax
import jax.numpy as jnp
from jax import lax
from jax.experimental import pallas as pl
from jax.experimental.pallas import tpu as pltpu

D_MODEL = 1024
HEAD_DIM = 64
N_HEADS = 8
N_KV_B = 2
W_MIX = N_HEADS * HEAD_DIM
W_KVB = N_KV_B * HEAD_DIM
D_FF = 2816
DILATIONS = (1, 4, 16)
BAND = 128
WIN_B = 128
ROT_DIM = 16
ROPE_THETA = 500000.0
EPS = 1e-6
SCALE = HEAD_DIM ** -0.5
PAST_LEN = 16384
LANES = 128
COLS = (0, 512, 1024, 1536, 2048, 2176, 2304)
VMEM_LIMIT = 60 * 1024 * 1024
FF_CHUNK = D_FF // 2
ATTN_CHUNK = BAND * DILATIONS[-1]
FF_SPLIT = 768
PAIRS = W_MIX // LANES
Q_ROWS = 8

_bf16 = jnp.bfloat16
_f32 = jnp.float32


def _params(n_axes):
    return pltpu.CompilerParams(dimension_semantics=("arbitrary",) * n_axes,
                                vmem_limit_bytes=VMEM_LIMIT)


def _const_spec(shape):
    nd = len(shape)
    return pl.BlockSpec(shape, lambda *_: (0,) * nd, pipeline_mode=pl.Buffered(1))


def _lane_lt64(shape):
    return lax.broadcasted_iota(jnp.int32, shape, len(shape) - 1) % LANES < HEAD_DIM


def _rms(x, g):
    return x * lax.rsqrt(jnp.mean(x * x, axis=-1, keepdims=True) + EPS) * g


def _head_norm(y, g2):
    lo = _lane_lt64(y.shape)
    sq = y * y
    s_lo = jnp.sum(jnp.where(lo, sq, 0.0), axis=-1, keepdims=True)
    s_hi = jnp.sum(jnp.where(lo, 0.0, sq), axis=-1, keepdims=True)
    ms = jnp.where(lo, s_lo, s_hi) * (1.0 / HEAD_DIM)
    return y * lax.rsqrt(ms + EPS) * g2


def _rope(y, cos2, sin2):
    lane = lax.broadcasted_iota(jnp.int32, y.shape, 1)
    swap = jnp.where(lane % HEAD_DIM < ROT_DIM, lane ^ (ROT_DIM // 2), lane)
    partner = jnp.take_along_axis(y, swap, axis=1)
    return y * cos2 + partner * sin2


def _rope_tables(pos):
    half = ROT_DIM // 2
    inv = jnp.exp(-math.log(ROPE_THETA) * jnp.arange(0, ROT_DIM, 2, dtype=_f32) / ROT_DIM)
    ang = pos.astype(_f32)[:, None] * inv[None, :]
    cos, sin = jnp.cos(ang), jnp.sin(ang)
    ones = jnp.ones((pos.shape[0], HEAD_DIM - ROT_DIM), _f32)
    c = jnp.concatenate([cos, cos, ones], axis=1)
    s = jnp.concatenate([-sin, sin, 0.0 * ones], axis=1)
    return jnp.concatenate([c, c], axis=1), jnp.concatenate([s, s], axis=1)


def _project(x, g_ref, w_ref, gains, cos2, sin2, scr_ref):
    gqa, gka, gqb, gkb = gains
    h = _rms(x, g_ref[...]).astype(_bf16)

    def seg(i):
        return jnp.dot(h, w_ref[:, COLS[i]:COLS[i + 1]], preferred_element_type=_f32)

    def norm_rope(y, g2, scale):
        outs = []
        for c in range(y.shape[1] // LANES):
            t = _rope(_head_norm(y[:, c * LANES:(c + 1) * LANES], g2), cos2, sin2)
            outs.append(t * scale if scale != 1.0 else t)
        return outs

    segs = [seg(i) for i in range(len(COLS) - 1)]
    for c, t in enumerate(norm_rope(segs[0], gqa[...], SCALE)):
        scr_ref[c] = t
    for c, t in enumerate(norm_rope(segs[1], gka[...], 1.0)):
        scr_ref[PAIRS + c] = t
    for c in range(PAIRS):
        scr_ref[2 * PAIRS + c] = segs[2][:, c * LANES:(c + 1) * LANES]
    qb = jnp.concatenate(norm_rope(segs[3], gqb[...], SCALE), axis=1)
    kb = norm_rope(segs[4], gkb[...], 1.0)[0]
    return qb, kb, segs[5]


def _dup_heads(y):
    lo = _lane_lt64(y.shape)
    sw = pltpu.roll(y, HEAD_DIM, 1)
    return jnp.concatenate([jnp.where(lo, y, sw), jnp.where(lo, sw, y)], axis=1)


def _proj_prompt_kernel(x_ref, g_ref, w_ref, gqa, gka, gqb, gkb, cos_ref, sin_ref,
                        *rest, tm, n_blocks, ca_blocks):
    (q1, k1, v1, q4, k4, v4, q16, k16, v16, qb_ref, kb_ref, vb_ref, ca_ref, cb_ref, scr_ref) = rest[-15:]
    j = pl.program_id(1)
    qb, kb, vb = _project(x_ref[0], g_ref, w_ref, (gqa, gka, gqb, gkb),
                          cos_ref[...], sin_ref[...], scr_ref)
    qb_ref[0] = qb.astype(_bf16)
    kb_ref[0] = _dup_heads(kb).astype(_bf16)
    vb_ref[0] = _dup_heads(vb).astype(_bf16)
    for d, (qo, ko, vo) in zip(DILATIONS, ((q1, k1, v1), (q4, k4, v4), (q16, k16, v16))):
        for r in range(d):
            rows = pl.ds(r, tm // d, stride=d) if d > 1 else slice(None)
            for i, out in enumerate((qo, ko, vo)):
                for c in range(PAIRS):
                    out[0, r, :, c * LANES:(c + 1) * LANES] = scr_ref[i * PAIRS + c, rows, :].astype(_bf16)

    @pl.when(j >= n_blocks - ca_blocks)
    def _():
        for kv in range(2):
            for c in range(PAIRS):
                ca_ref[0, kv, c * LANES:(c + 1) * LANES, :] = scr_ref[(1 + kv) * PAIRS + c].T

    @pl.when(j == n_blocks - 1)
    def _():
        cb_ref[0, 0] = kb[tm - WIN_B:, :].T
        cb_ref[0, 1] = vb[tm - WIN_B:, :].T


def _proj_prompt(x, g, w, gains, cos2, sin2, cache_shapes, prev, layer, tm=512):
    nb, s, _ = x.shape
    n_blocks = s // tm
    ca_blocks = cache_shapes[0][-1] // tm
    row = lambda n, j: (n, j, 0)
    lay = lambda n, j: (n, 0, j, 0)
    vec = lambda w_: _const_spec((1, w_))
    in_specs = [pl.BlockSpec((1, tm, D_MODEL), row), vec(D_MODEL), _const_spec((D_MODEL, COLS[-1])),
                vec(LANES), vec(LANES), vec(LANES), vec(LANES),
                pl.BlockSpec((tm, LANES), lambda n, j: (j, 0)), pl.BlockSpec((tm, LANES), lambda n, j: (j, 0))]
    args = [x, g, w, *gains, cos2, sin2]
    aliases = {}
    if prev is not None:
        aliases = {len(in_specs): 12, len(in_specs) + 1: 13}
        in_specs += [pl.BlockSpec(memory_space=pl.ANY), pl.BlockSpec(memory_space=pl.ANY)]
        args += list(prev)
    out_shape, out_specs = [], []
    for d in DILATIONS:
        for _ in range(3):
            out_shape.append(jax.ShapeDtypeStruct((nb, d, s // d, W_MIX), _bf16))
            out_specs.append(pl.BlockSpec((1, d, tm // d, W_MIX), lay))
    out_shape += [jax.ShapeDtypeStruct((nb, s, W_MIX), _bf16),
                  jax.ShapeDtypeStruct((nb, s, 2 * W_KVB), _bf16),
                  jax.ShapeDtypeStruct((nb, s, 2 * W_KVB), _bf16),
                  jax.ShapeDtypeStruct(cache_shapes[0], _f32),
                  jax.ShapeDtypeStruct(cache_shapes[1], _f32)]
    first = n_blocks - ca_blocks
    out_specs += [pl.BlockSpec((1, tm, W_MIX), row),
                  pl.BlockSpec((1, tm, 2 * W_KVB), row),
                  pl.BlockSpec((1, tm, 2 * W_KVB), row),
                  pl.BlockSpec((None, 1, 2, W_MIX, tm),
                               lambda n, j: (layer, n, 0, 0, jnp.maximum(j - first, 0))),
                  pl.BlockSpec((None, 1, 2, W_KVB, WIN_B), lambda n, j: (layer, n, 0, 0, 0))]
    return pl.pallas_call(
        functools.partial(_proj_prompt_kernel, tm=tm, n_blocks=n_blocks, ca_blocks=ca_blocks),
        grid=(nb, n_blocks), in_specs=in_specs, out_specs=out_specs, out_shape=out_shape,
        scratch_shapes=[pltpu.VMEM((3 * PAIRS, tm, LANES), _f32)],
        input_output_aliases=aliases,
        compiler_params=_params(2), name=f"proj_prompt_l{layer}",
    )(*args)


def _band_attn_kernel(*refs, d, n_sub, max_dist, swa, with_lse):
    it = iter(refs)
    q_ref, kc_ref, kp_ref, vc_ref, vp_ref = (next(it) for _ in range(5))
    sink_ref = next(it) if swa else None
    o_ref = next(it)
    lse_ref = next(it) if with_lse else None
    kcat, vcat = next(it), next(it)
    c = pl.program_id(1)
    kcat[:, 0:BAND, :] = kp_ref[0]
    kcat[:, BAND:, :] = kc_ref[0]
    vcat[:, 0:BAND, :] = vp_ref[0]
    vcat[:, BAND:, :] = vc_ref[0]

    qi = lax.broadcasted_iota(jnp.int32, (2 * BAND, BAND), 0) % BAND
    kj = lax.broadcasted_iota(jnp.int32, (2 * BAND, BAND), 1)
    from_prev = kj > qi
    diag = kj == qi
    has_diag = max_dist == BAND
    lo_q = _lane_lt64((BAND, LANES))
    top = lax.broadcasted_iota(jnp.int32, (2 * BAND, 1), 0) < BAND

    kcols = [(p // 2) * LANES if swa else p * LANES for p in range(PAIRS)]
    ones = jnp.ones((2 * BAND, LANES), _bf16)

    def body(idx, carry):
        r = idx // n_sub
        sb = idx % n_sub
        row0 = pl.multiple_of(sb * BAND, BAND)
        prev_bias = jnp.where((c == 0) & (sb == 0), -jnp.inf, 0.0).astype(_f32)
        q_blk = q_ref[0, r, pl.ds(row0, BAND), :]

        def score_dot(p):
            q2 = q_blk[:, p * LANES:(p + 1) * LANES]
            zero = jnp.zeros_like(q2)
            qs = jnp.concatenate([jnp.where(lo_q, q2, zero), jnp.where(lo_q, zero, q2)], axis=0)
            k2 = kcat[r, pl.ds(row0, 2 * BAND), kcols[p]:kcols[p] + LANES]
            return lax.dot_general(qs, k2, (((1,), (1,)), ((), ())), preferred_element_type=_f32)

        def softmax(p, s2):
            s_prev = s2[:, :BAND] + prev_bias
            s = jnp.where(from_prev, s_prev, s2[:, BAND:])
            m = jnp.max(s, axis=-1, keepdims=True)
            if has_diag:
                s_far = jnp.sum(jnp.where(diag, s_prev, 0.0), axis=-1, keepdims=True)
                m = jnp.maximum(m, s_far)
            if swa:
                m = jnp.maximum(m, jnp.where(top, sink_ref[2 * p], sink_ref[2 * p + 1]))
            pe = jnp.exp((s - m).astype(_bf16))
            zero = jnp.zeros_like(pe)
            p_prev = jnp.where(from_prev, pe, zero)
            if has_diag:
                p_prev = jnp.where(diag, jnp.exp((s_far - m).astype(_bf16)), p_prev)
            p_cur = jnp.where(from_prev, zero, pe)
            return jnp.concatenate([p_prev, p_cur], axis=1), m

        def value_dot(p, probs):
            v2 = vcat[r, pl.ds(row0, 2 * BAND), kcols[p]:kcols[p] + LANES]
            return jnp.dot(probs, jnp.concatenate([v2, ones], axis=1), preferred_element_type=_f32)

        def finish(p, both, m):
            den = both[:, LANES:]
            if swa:
                den = den + jnp.exp(jnp.where(top, sink_ref[2 * p], sink_ref[2 * p + 1]) - m)
            acc = both[:, :LANES] * (1.0 / den)
            out_rows = (pl.ds(r + d * row0, BAND, stride=d) if d > 1 else pl.ds(row0, BAND))
            o_ref[0, p, out_rows, :] = jnp.where(lo_q, acc[:BAND], acc[BAND:])
            if with_lse:
                lse = m + jnp.log(den)
                lse_ref[0, p, out_rows, :] = jnp.where(lo_q, lse[:BAND], lse[BAND:])

        scores = [score_dot(p) for p in range(PAIRS)]
        soft = [softmax(p, scores[p]) for p in range(PAIRS)]
        boths = [value_dot(p, soft[p][0]) for p in range(PAIRS)]
        for p in range(PAIRS):
            finish(p, boths[p], soft[p][1])
        return carry

    lax.fori_loop(0, d * n_sub, body, 0)


def _band_attn(q, k, v, d, chunk, max_dist, sinks=None, layer=0):
    assert max_dist in (BAND - 1, BAND)
    nb, _, ls, wk = k.shape
    s = ls * d
    rows = chunk // d
    n_sub = rows // BAND
    swa = sinks is not None
    cur = lambda n, c: (n, 0, c, 0)
    prev = lambda n, c: (n, 0, jnp.maximum(c * n_sub - 1, 0), 0)
    in_specs = [pl.BlockSpec((1, d, rows, W_MIX), cur),
                pl.BlockSpec((1, d, rows, wk), cur), pl.BlockSpec((1, d, BAND, wk), prev),
                pl.BlockSpec((1, d, rows, wk), cur), pl.BlockSpec((1, d, BAND, wk), prev)]
    args = [q, k, k, v, v]
    if swa:
        in_specs.append(pl.BlockSpec(memory_space=pltpu.SMEM))
        args.append(sinks)
    nat = pl.BlockSpec((1, PAIRS, chunk, LANES), lambda n, c: (n, 0, c, 0))
    out_shape = [jax.ShapeDtypeStruct((nb, PAIRS, s, LANES), _f32)]
    out_specs = [nat]
    if not swa:
        out_shape.append(jax.ShapeDtypeStruct((nb, PAIRS, s, LANES), _f32))
        out_specs.append(nat)
    return pl.pallas_call(
        functools.partial(_band_attn_kernel, d=d, n_sub=n_sub, max_dist=max_dist, swa=swa,
                          with_lse=not swa),
        grid=(nb, s // chunk), in_specs=in_specs, out_specs=out_specs, out_shape=out_shape,
        scratch_shapes=[pltpu.VMEM((d, rows + BAND, wk), _bf16), pltpu.VMEM((d, rows + BAND, wk), _bf16)],
        compiler_params=_params(2), name=f"attn_d{d}_{'swa' if swa else 'dil'}_l{layer}",
    )(*args)


def _mix_residual(x, o_refs, l_refs, ob_ref, ga_ref, gb_ref, wo_ref):
    wide = lambda ref: jnp.concatenate([ref[0, p] for p in range(PAIRS)], axis=1)
    if len(o_refs) > 1:
        lses = [wide(l) for l in l_refs]
        top = functools.reduce(jnp.maximum, lses)
        ws = [jnp.exp(l - top) for l in lses]
        den = functools.reduce(jnp.add, ws)
        oa = functools.reduce(jnp.add, [w * wide(o) for w, o in zip(ws, o_refs)]) / den
    else:
        oa = wide(o_refs[0])
    mix_a = _rms(oa, ga_ref[...]).astype(_bf16)
    mix_b = _rms(wide(ob_ref), gb_ref[...]).astype(_bf16)
    mix = jnp.concatenate([mix_a, mix_b], axis=1)
    return x + jnp.dot(mix, wo_ref[...], preferred_element_type=_f32)


def _ffn_slice(h, wgu_ref, wd_ref, lo, hi):
    gate = jnp.dot(h, wgu_ref[:, lo:hi], preferred_element_type=_f32)
    up = jnp.dot(h, wgu_ref[:, D_FF + lo:D_FF + hi], preferred_element_type=_f32)
    act = (gate * jax.nn.sigmoid(gate) * up).astype(_bf16)
    return jnp.dot(act, wd_ref[lo:hi, :], preferred_element_type=_f32)


def _out_ffn_kernel(*refs, n_groups):
    it = iter(refs)
    x_ref = next(it)
    o_refs = [next(it) for _ in range(n_groups)]
    l_refs = [next(it) for _ in range(n_groups)] if n_groups > 1 else []
    ob_ref, ga_ref, gb_ref, wo_ref, gf_ref, wgu_ref, wd_ref, y_ref = (next(it) for _ in range(8))
    x1 = _mix_residual(x_ref[0], o_refs, l_refs, ob_ref, ga_ref, gb_ref, wo_ref)
    h = _rms(x1, gf_ref[...]).astype(_bf16)
    y_ref[0] = x1
    for lo in range(0, D_FF, FF_CHUNK):
        y_ref[0] += _ffn_slice(h, wgu_ref, wd_ref, lo, lo + FF_CHUNK)


def _out_ffn(x, o_list, lse_list, ob, ga, gb, wo, gf, wgu, wd, tm, name):
    nb, s, _ = x.shape
    row = lambda n, j: (n, j, 0)
    act = lambda w_: pl.BlockSpec((1, tm, w_), row)
    mixer = pl.BlockSpec((1, PAIRS, tm, LANES), lambda n, j: (n, 0, j, 0))
    n_groups = len(o_list)
    in_specs = ([act(D_MODEL)] + [mixer] * (n_groups + len(lse_list) + 1)
                + [_const_spec((1, W_MIX)), _const_spec((1, W_MIX)), _const_spec((D_MODEL, D_MODEL)),
                   _const_spec((1, D_MODEL)), _const_spec((D_MODEL, 2 * D_FF)), _const_spec((D_FF, D_MODEL))])
    return pl.pallas_call(
        functools.partial(_out_ffn_kernel, n_groups=n_groups),
        grid=(nb, s // tm), in_specs=in_specs, out_specs=act(D_MODEL),
        out_shape=jax.ShapeDtypeStruct(x.shape, _f32),
        compiler_params=_params(2), name=name,
    )(x, *o_list, *lse_list, ob, ga, gb, wo, gf, wgu, wd)


def _proj_sample_kernel(x_ref, g_ref, w_ref, gqa, gka, gqb, gkb, cos_ref, sin_ref,
                        qa_ref, qb_ref, kvt_ref, scr_ref):
    qb, kb, vb = _project(x_ref[...], g_ref, w_ref, (gqa, gka, gqb, gkb),
                          cos_ref[...], sin_ref[...], scr_ref)
    qb_ref[...] = qb
    for c in range(PAIRS):
        qa_ref[:, c * LANES:(c + 1) * LANES] = scr_ref[c]
        kvt_ref[c * LANES:(c + 1) * LANES, :] = scr_ref[PAIRS + c].T
        kvt_ref[W_MIX + c * LANES:W_MIX + (c + 1) * LANES, :] = scr_ref[2 * PAIRS + c].T
    kvt_ref[2 * W_MIX:2 * W_MIX + W_KVB, :] = kb.T
    kvt_ref[2 * W_MIX + W_KVB:, :] = vb.T


def _proj_sample(x, g, w, gains, cos2, sin2, layer):
    n_tok = x.shape[0]
    full = lambda shape: pl.BlockSpec(shape, lambda i: (0,) * len(shape))
    in_specs = [full((n_tok, D_MODEL)), full((1, D_MODEL)), full((D_MODEL, COLS[-1]))] + \
               [full((1, LANES))] * 4 + [full((n_tok, LANES))] * 2
    out_shape = [jax.ShapeDtypeStruct((n_tok, W_MIX), _f32), jax.ShapeDtypeStruct((n_tok, W_MIX), _f32),
                 jax.ShapeDtypeStruct((2 * W_MIX + 2 * W_KVB, n_tok), _f32)]
    out_specs = [full((n_tok, W_MIX)), full((n_tok, W_MIX)), full((2 * W_MIX + 2 * W_KVB, n_tok))]
    return pl.pallas_call(
        _proj_sample_kernel, grid=(1,), in_specs=in_specs, out_specs=out_specs, out_shape=out_shape,
        scratch_shapes=[pltpu.VMEM((3 * PAIRS, n_tok, LANES), _f32)],
        compiler_params=_params(1), name=f"proj_sample_l{layer}",
    )(x, g, w, *gains, cos2, sin2)


def _sample_weights(t_len, la, lb):
    def mult(dist):
        m = np.zeros(dist.shape, np.float32)
        for d in DILATIONS:
            m += ((dist >= 0) & (dist % d == 0) & (dist <= BAND * d)).astype(np.float32)
        return m
    t = np.arange(t_len)[:, None]
    w_old = mult(la + t - np.arange(la)[None, :])
    u = np.arange(LANES)[None, :] - (LANES - t_len)
    d_new = np.where(u >= 0, t - u, -1)
    w_new = mult(d_new)
    d_old_b = lb + t - np.arange(lb)[None, :]
    b_old = ((d_old_b >= 0) & (d_old_b < WIN_B)).astype(np.float32)
    b_new = ((d_new >= 0) & (d_new < WIN_B)).astype(np.float32)
    def table(old, new):
        a = np.concatenate([old, new], axis=1)
        return jnp.asarray(np.concatenate([a, np.ones((Q_ROWS - t_len, a.shape[1]), np.float32)]))
    return table(w_old, w_new), table(b_old, b_new)


def _ffn_sample_kernel(x_ref, o1_ref, o4_ref, o16_ref, l1_ref, l4_ref, l16_ref, pb_ref, ga_ref, gb_ref,
                       wo_ref, gf_ref, wgu_ref, wd_ref,
                       ca_ref, cb_ref, kvt_ref, qa_ref, qb_ref, wa_ref, wb_ref, sink_ref, *rest,
                       t_len, per_tile):
    y_ref, na_ref, nb_ref, oa_ref, ob_ref, h_scr, p_scr, d_scr = rest[-8:]
    b = pl.program_id(0)
    half = pl.program_id(1)
    la = ca_ref.shape[-1]
    lo_a = la - LANES
    tail = lax.broadcasted_iota(jnp.int32, (1, LANES), 1) >= LANES - t_len
    nt = (((1,), (1,)), ((), ()))

    def new_columns():
        shift = (LANES - t_len) - t_len * (b % per_tile)
        return pltpu.roll(kvt_ref[...], jnp.where(shift < 0, shift + LANES, shift), 1)

    def shifted(old, new_rows):
        rolled = pltpu.roll(old, old.shape[-1] - t_len, 1)
        last = jnp.where(tail, new_rows, rolled[:, -LANES:])
        return rolled, last

    def shift_half(new_rows):
        rolled, last = shifted(ca_ref[...], new_rows)
        na_ref[:, 0:lo_a] = rolled[:, 0:lo_a]
        na_ref[:, lo_a:] = last

    def keys(old, new_rows):
        return jnp.concatenate([old, new_rows], axis=1).astype(_bf16)

    @pl.when(half == 0)
    def _():
        x1 = _mix_residual(x_ref[0], (o1_ref, o4_ref, o16_ref), (l1_ref, l4_ref, l16_ref), pb_ref,
                           ga_ref, gb_ref, wo_ref)
        h = _rms(x1, gf_ref[...]).astype(_bf16)
        h_scr[...] = h
        y_ref[0] = x1 + _ffn_slice(h, wgu_ref, wd_ref, 0, FF_SPLIT)

        new_t = new_columns()
        shift_half(new_t[0:W_MIX, :])
        wa = wa_ref[...]
        qa = qa_ref[0]
        for h in range(N_HEADS):
            hs = slice(h * HEAD_DIM, (h + 1) * HEAD_DIM)
            q = qa[:, hs].astype(_bf16)
            s = jnp.dot(q, keys(ca_ref[hs, :], new_t[hs, :]), preferred_element_type=_f32)
            s = jnp.where(wa > 0, s, -jnp.inf)
            m = jnp.max(s, axis=-1, keepdims=True)
            p = wa * jnp.exp(s - m)
            p_scr[h] = p
            d_scr[h] = jnp.broadcast_to(1.0 / jnp.sum(p, axis=-1, keepdims=True), (Q_ROWS, LANES))

        wb = wb_ref[...]
        qb = qb_ref[0]
        base = 2 * W_MIX
        for kv in range(2):
            _, last = shifted(cb_ref[kv], new_t[base + kv * W_KVB:base + (kv + 1) * W_KVB, :])
            nb_ref[kv] = last
        outs = []
        for h in range(N_HEADS):
            g = h // (N_HEADS // N_KV_B)
            gs = slice(g * HEAD_DIM, (g + 1) * HEAD_DIM)
            q = qb[:, h * HEAD_DIM:(h + 1) * HEAD_DIM].astype(_bf16)
            k = keys(cb_ref[0, gs, :], new_t[base + g * HEAD_DIM:base + (g + 1) * HEAD_DIM, :])
            s = jnp.dot(q, k, preferred_element_type=_f32)
            s = jnp.where(wb > 0, s, -jnp.inf)
            sink = sink_ref[h]
            m = jnp.maximum(jnp.max(s, axis=-1, keepdims=True), sink)
            p = jnp.exp(s - m)
            den = jnp.sum(p, axis=-1, keepdims=True) + jnp.exp(sink - m)
            v = keys(cb_ref[1, gs, :],
                     new_t[base + W_KVB + g * HEAD_DIM:base + W_KVB + (g + 1) * HEAD_DIM, :])
            acc = lax.dot_general(p.astype(_bf16), v, nt, preferred_element_type=_f32)
            outs.append(acc * (1.0 / den))
        ob_ref[0] = jnp.concatenate(outs, axis=1)

    @pl.when(half == 1)
    def _():
        y_ref[0] += _ffn_slice(h_scr[...], wgu_ref, wd_ref, FF_SPLIT, D_FF)

        new_t = new_columns()
        shift_half(new_t[W_MIX:2 * W_MIX, :])
        outs = []
        for h in range(N_HEADS):
            hs = slice(h * HEAD_DIM, (h + 1) * HEAD_DIM)
            v = keys(ca_ref[hs, :], new_t[W_MIX + h * HEAD_DIM:W_MIX + (h + 1) * HEAD_DIM, :])
            acc = lax.dot_general(p_scr[h].astype(_bf16), v, nt, preferred_element_type=_f32)
            outs.append(acc * d_scr[h][:, 0:1])
        oa_ref[0] = jnp.concatenate(outs, axis=1)


def _ffn_sample(x, o_list, lse_list, pb, post, cache_a, cache_b, kvt, qa, qb, weights, sinks, prev, layer,
                t_len):
    _, nbatch, _, _, la = cache_a.shape
    lb = cache_b.shape[-1]
    nb, s, _ = x.shape
    assert (nb * s) % nbatch == 0
    tm = nb * s // nbatch
    assert s % tm == 0 and tm % 16 == 0
    per_seq = s // tm
    per_tile = LANES // t_len
    row = lambda b, h: (b // per_seq, b % per_seq, 0)
    mixer = pl.BlockSpec((1, PAIRS, tm, LANES), lambda b, h: (b // per_seq, 0, b % per_seq, 0))
    tok = lambda b, h: (b, 0, 0)
    const = lambda shape: pl.BlockSpec(shape, lambda b, h: (0,) * len(shape))
    in_specs = ([pl.BlockSpec((1, tm, D_MODEL), row)] + [mixer] * 7
                + [_const_spec((1, W_MIX)), _const_spec((1, W_MIX)), _const_spec((D_MODEL, D_MODEL)),
                   _const_spec((1, D_MODEL)), _const_spec((D_MODEL, 2 * D_FF)), _const_spec((D_FF, D_MODEL))]
                + [pl.BlockSpec((None, None, None, W_MIX, la), lambda b, h: (layer, b, h, 0, 0)),
                   pl.BlockSpec((None, None, 2, W_KVB, lb), lambda b, h: (layer, b, 0, 0, 0)),
                   pl.BlockSpec((kvt.shape[0], LANES), lambda b, h: (0, b // per_tile)),
                   pl.BlockSpec((1, Q_ROWS, W_MIX), tok), pl.BlockSpec((1, Q_ROWS, W_MIX), tok),
                   const((Q_ROWS, la + LANES)), const((Q_ROWS, lb + LANES)),
                   pl.BlockSpec(memory_space=pltpu.SMEM)])
    args = [x, *o_list, *lse_list, pb, *post, cache_a, cache_b, kvt, qa, qb, *weights, sinks]
    aliases = {}
    if prev is not None:
        aliases = {len(in_specs): 1, len(in_specs) + 1: 2}
        in_specs += [pl.BlockSpec(memory_space=pl.ANY), pl.BlockSpec(memory_space=pl.ANY)]
        args += list(prev)
    out_shape = [jax.ShapeDtypeStruct(x.shape, _f32),
                 jax.ShapeDtypeStruct(cache_a.shape, _f32), jax.ShapeDtypeStruct(cache_b.shape, _f32),
                 jax.ShapeDtypeStruct((nbatch, Q_ROWS, W_MIX), _f32),
                 jax.ShapeDtypeStruct((nbatch, Q_ROWS, W_MIX), _f32)]
    out_specs = [pl.BlockSpec((1, tm, D_MODEL), row),
                 pl.BlockSpec((None, None, None, W_MIX, la), lambda b, h: (layer, b, h, 0, 0)),
                 pl.BlockSpec((None, None, 2, W_KVB, lb), lambda b, h: (layer, b, 0, 0, 0)),
                 pl.BlockSpec((1, Q_ROWS, W_MIX), tok), pl.BlockSpec((1, Q_ROWS, W_MIX), tok)]
    return pl.pallas_call(
        functools.partial(_ffn_sample_kernel, t_len=t_len, per_tile=per_tile),
        grid=(nbatch, 2), in_specs=in_specs, out_specs=out_specs, out_shape=out_shape,
        scratch_shapes=[pltpu.VMEM((tm, D_MODEL), _bf16),
                        pltpu.VMEM((N_HEADS, Q_ROWS, la + LANES), _f32),
                        pltpu.VMEM((N_HEADS, Q_ROWS, LANES), _f32)],
        input_output_aliases=aliases,
        compiler_params=_params(2), name=f"ffn_prompt_attn_sample_l{layer}",
    )(*args)


def _feature_major(c):
    l, b, rows, two, h, hd = c.shape
    return c.transpose(0, 1, 3, 4, 5, 2).reshape(l, b, two, h * hd, rows)


def _row_major(c, h):
    l, b, two, w, rows = c.shape
    return c.reshape(l, b, two, h, w // h, rows).transpose(0, 1, 5, 2, 3, 4)


def kernel(x_prompt, x_sample, cache_a, cache_b, attn_norm, w_in, q_norm_a, k_norm_a, q_norm_b,
           k_norm_b, sinks_b, out_norm_a, out_norm_b, w_out, ffn_norm, w_gate_up, w_down):
    depth = w_in.shape[0]
    nb, s_len, _ = x_prompt.shape
    dec_b, t_len, _ = x_sample.shape
    la_p, lb_p = min(BAND * DILATIONS[-1], s_len), min(WIN_B, s_len)
    assert s_len % (BAND * DILATIONS[-1]) == 0 and la_p == BAND * DILATIONS[-1] and lb_p == WIN_B
    assert LANES % t_len == 0 and (dec_b * t_len) % LANES == 0

    cos_p, sin_p = _rope_tables(jnp.arange(s_len, dtype=jnp.int32))
    pos_s = PAST_LEN + (jnp.arange(dec_b * t_len, dtype=jnp.int32) % t_len)
    cos_s, sin_s = _rope_tables(pos_s)
    ca_fm, cb_fm = _feature_major(cache_a), _feature_major(cache_b)
    la, lb = ca_fm.shape[-1], cb_fm.shape[-1]
    weights = _sample_weights(t_len, la, lb)

    shapes_p = ((depth, nb, 2, W_MIX, la_p), (depth, nb, 2, W_KVB, lb_p))
    prev_p = prev_s = None
    pad_q = lambda q: jnp.pad(q.reshape(dec_b, t_len, W_MIX), ((0, 0), (0, Q_ROWS - t_len), (0, 0)))

    xp = x_prompt
    xs = x_sample.reshape(dec_b * t_len, D_MODEL)
    two = lambda g: jnp.concatenate([g, g])[None, :]
    for layer in range(depth):
        g_attn = attn_norm[layer][None, :]
        w = w_in[layer].astype(_bf16)
        gains = (two(q_norm_a[layer]), two(k_norm_a[layer]), two(q_norm_b[layer]), two(k_norm_b[layer]))
        post = (out_norm_a[layer][None, :], out_norm_b[layer][None, :], w_out[layer].astype(_bf16),
                ffn_norm[layer][None, :], w_gate_up[layer].astype(_bf16), w_down[layer].astype(_bf16))
        sinks = sinks_b[layer]

        (q1, k1, v1, q4, k4, v4, q16, k16, v16, qb, kb, vb, new_a_p, new_b_p) = _proj_prompt(
            xp, g_attn, w, gains, cos_p, sin_p, shapes_p, prev_p, layer)
        prev_p = (new_a_p, new_b_p)
        o_list, lse_list = [], []
        for d, (q, k, v) in zip(DILATIONS, ((q1, k1, v1), (q4, k4, v4), (q16, k16, v16))):
            o, lse = _band_attn(q, k, v, d, ATTN_CHUNK, BAND, layer=layer)
            o_list.append(o)
            lse_list.append(lse)
        (ob,) = _band_attn(qb[:, None], kb[:, None], vb[:, None], 1, ATTN_CHUNK, WIN_B - 1, sinks=sinks,
                           layer=layer)
        qa_s, qb_s, kvt = _proj_sample(xs, g_attn, w, gains, cos_s, sin_s, layer)
        xp, new_a_s, new_b_s, oa_s, ob_s = _ffn_sample(
            xp, o_list, lse_list, ob, post, ca_fm, cb_fm, kvt, pad_q(qa_s), pad_q(qb_s), weights, sinks,
            prev_s, layer, t_len)
        prev_s = (new_a_s, new_b_s)
        unpad = lambda o: o[:, :t_len].reshape(dec_b * t_len, PAIRS, LANES).transpose(1, 0, 2)[None]
        xs = _out_ffn(xs[None], [unpad(oa_s)], [], unpad(ob_s), *post, tm=dec_b * t_len,
                      name=f"out_ffn_sample_l{layer}")[0]

    return (xp, xs.reshape(dec_b, t_len, D_MODEL), _row_major(new_a_p, N_HEADS), _row_major(new_b_p, N_KV_B),
            _row_major(new_a_s, N_HEADS), _row_major(new_b_s, N_KV_B))
```
